```python
import math
import jax, jax.numpy as jnp
from jax import lax
import numpy as np

D_MODEL = 2048
BATCH = 1
SEQ = 16384
DEPTH = 1
DEC_BATCH = 4
DEC_SEQ = 2048
PAST_LEN = 128

N_META = 16
CHUNK = 64
DN_HEADS = 8
DN_HEAD_DIM = 128
DN_WIDTH = DN_HEADS * DN_HEAD_DIM
CONV_K = 5
FN_GROUPS = 4
FN_GROUP_DIM = 256
FN_WIDTH = FN_GROUPS * FN_GROUP_DIM
N_BRANCH = 2
DEEPNORM_ALPHA = (2 * DEPTH) ** 0.25
DEEPNORM_BETA = (8 * DEPTH) ** -0.25
LN_EPS = 1e-5
RMS_EPS = 1e-6
L2_EPS = 1e-6

COL_Q = 0
COL_K = COL_Q + DN_WIDTH
COL_V = COL_K + DN_WIDTH
COL_ZA = COL_V + DN_WIDTH
COL_B = COL_ZA + DN_WIDTH
COL_A = COL_B + 2 * DN_HEADS
COL_F = COL_A + 2 * DN_HEADS
COL_ZF = COL_F + FN_WIDTH
COL_G = COL_ZF + FN_WIDTH
D_IN = COL_G + N_BRANCH * D_MODEL

kernel_name = "hybrid_deltanet_fnet_encoder"


def layer_norm(x, g, b):
    xf = x.astype(jnp.float32)
    mu = jnp.mean(xf, axis=-1, keepdims=True)
    var = jnp.mean(jnp.square(xf - mu), axis=-1, keepdims=True)
    return ((xf - mu) * lax.rsqrt(var + LN_EPS) * g.astype(jnp.float32) + b.astype(jnp.float32)).astype(x.dtype)


def l2norm(x):
    return x * lax.rsqrt(jnp.sum(x * x, axis=-1, keepdims=True) + L2_EPS)


def centred_depthwise_conv(x, w):
    c = x.shape[-1]
    return lax.conv_general_dilated(
        x, w[:, None, :].astype(x.dtype), window_strides=(1,),
        padding=[(CONV_K // 2, CONV_K // 2)],
        dimension_numbers=("NWC", "WIO", "NWC"), feature_group_count=c)


def to_chunk_layout(t):
    pad = jnp.zeros((t.shape[0], CHUNK - N_META) + t.shape[2:], t.dtype)
    return jnp.concatenate([t[:, :N_META], pad, t[:, N_META:]], axis=1)


def from_chunk_layout(t):
    return jnp.concatenate([t[:, :N_META], t[:, CHUNK:]], axis=1)


def chunk_gated_delta_rule(q, k, v, g, beta):
    B, Tp, H, dk = q.shape
    dv = v.shape[-1]
    n = Tp // CHUNK
    qc = jnp.transpose(q.reshape(B, n, CHUNK, H, dk), (1, 0, 3, 2, 4))
    kc = jnp.transpose(k.reshape(B, n, CHUNK, H, dk), (1, 0, 3, 2, 4))
    vc = jnp.transpose(v.reshape(B, n, CHUNK, H, dv), (1, 0, 3, 2, 4))
    gch = jnp.transpose(g.reshape(B, n, CHUNK, H), (1, 0, 3, 2))
    bch = jnp.transpose(beta.reshape(B, n, CHUNK, H), (1, 0, 3, 2))

    gc = jnp.cumsum(gch, axis=-1)
    idx = jnp.arange(CHUNK)
    incl = idx[:, None] >= idx[None, :]
    strict = idx[:, None] > idx[None, :]
    decay = jnp.exp(jnp.where(incl, gc[..., :, None] - gc[..., None, :], -jnp.inf))

    kb = kc * bch[..., None]
    lower = jnp.where(strict, jnp.einsum('nbhcd,nbhsd->nbhcs', kb, kc) * decay, 0.0)
    a_mat = lower + jnp.eye(CHUNK, dtype=jnp.float32)
    u = lax.linalg.triangular_solve(a_mat, vc * bch[..., None], left_side=True, lower=True, unit_diagonal=True)
    w = lax.linalg.triangular_solve(a_mat, kb * jnp.exp(gc)[..., None], left_side=True, lower=True, unit_diagonal=True)

    qk = jnp.einsum('nbhcd,nbhsd->nbhcs', qc, kc) * decay
    q_dec = qc * jnp.exp(gc)[..., None]
    k_dec = kc * jnp.exp(gc[..., -1:] - gc)[..., None]
    g_last = jnp.exp(gc[..., -1])

    def step(state, xs):
        qk_n, qd_n, kd_n, u_n, w_n, gl_n = xs
        v_new = u_n - jnp.einsum('bhcd,bhde->bhce', w_n, state)
        o = jnp.einsum('bhcd,bhde->bhce', qd_n, state) + jnp.einsum('bhcs,bhse->bhce', qk_n, v_new)
        state = state * gl_n[..., None, None] + jnp.einsum('bhcd,bhce->bhde', kd_n, v_new)
        return state, o

    s0 = jnp.zeros((B, H, dk, dv), jnp.float32)
    _, o = lax.scan(step, s0, (qk, q_dec, k_dec, u, w, g_last))
    return jnp.transpose(o, (1, 0, 3, 2, 4)).reshape(B, Tp, H, dv)


def encoder_layer(h, w_in, b_gate, conv_w, a_log, dt_bias, dn_norm_w, w_proj_a, w_proj_f, w_out, ln_g, ln_b):
    B, T, _ = h.shape
    f32 = jnp.float32
    proj = jnp.einsum('btd,de->bte', h, w_in)

    qkv = jax.nn.silu(centred_depthwise_conv(proj[..., COL_Q:COL_ZA], conv_w)).astype(f32)
    q = l2norm(qkv[..., :DN_WIDTH].reshape(B, T, DN_HEADS, DN_HEAD_DIM)) * (DN_HEAD_DIM ** -0.5)
    k = l2norm(qkv[..., DN_WIDTH:2 * DN_WIDTH].reshape(B, T, DN_HEADS, DN_HEAD_DIM))
    v = qkv[..., 2 * DN_WIDTH:].reshape(B, T, DN_HEADS, DN_HEAD_DIM)
    beta = jax.nn.sigmoid(proj[..., COL_B:COL_A].astype(f32)).reshape(B, T, 2, DN_HEADS)
    g = -jnp.exp(a_log.astype(f32)) * jax.nn.softplus(
        proj[..., COL_A:COL_F].astype(f32).reshape(B, T, 2, DN_HEADS) + dt_bias.astype(f32))

    qp, kp, vp, bp, gp = (to_chunk_layout(t) for t in (q, k, v, beta, g))
    o_fwd = chunk_gated_delta_rule(qp, kp, vp, gp[:, :, 0], bp[:, :, 0])
    rev = lambda t: jnp.flip(t, axis=1)
    o_bwd = rev(chunk_gated_delta_rule(rev(qp), rev(kp), rev(vp), rev(gp[:, :, 1]), rev(bp[:, :, 1])))
    o = from_chunk_layout(o_fwd + o_bwd)

    o = o * lax.rsqrt(jnp.mean(o * o, axis=-1, keepdims=True) + RMS_EPS) * dn_norm_w.astype(f32)
    z_a = proj[..., COL_ZA:COL_B].astype(f32).reshape(B, T, DN_HEADS, DN_HEAD_DIM)
    y_a = (o * jax.nn.silu(z_a)).reshape(B, T, DN_WIDTH).astype(h.dtype)

    f = proj[..., COL_F:COL_ZF].astype(f32).reshape(B, T, FN_GROUPS, FN_GROUP_DIM)
    f = jnp.real(jnp.fft.fft2(f, axes=(1, 3), norm='ortho')).reshape(B, T, FN_WIDTH)
    y_f = (f * jax.nn.silu(proj[..., COL_ZF:COL_G].astype(f32))).astype(h.dtype)

    gates = jax.nn.sigmoid((proj[..., COL_G:] + b_gate).astype(f32))
    m = (gates[..., :D_MODEL] * jnp.einsum('bte,ed->btd', y_a, w_proj_a).astype(f32)
         + gates[..., D_MODEL:] * jnp.einsum('bte,ed->btd', y_f, w_proj_f).astype(f32))
    out = jnp.einsum('btd,de->bte', m.astype(h.dtype), w_out)

    return layer_norm(DEEPNORM_ALPHA * h + out, ln_g, ln_b)


def encode(x, meta_tokens, ln_in_g, ln_in_b, w_in, b_gate, conv_w, a_log, dt_bias, dn_norm_w,
           w_proj_a, w_proj_f, w_out, ln_g, ln_b):
    B = x.shape[0]
    meta = jnp.broadcast_to(meta_tokens[None].astype(x.dtype), (B, N_META, D_MODEL))
    h = layer_norm(jnp.concatenate([meta, x], axis=1), ln_in_g, ln_in_b)
    for i in range(DEPTH):
        h = encoder_layer(h, w_in[i], b_gate[i], conv_w[i], a_log[i], dt_bias[i], dn_norm_w[i],
                          w_proj_a[i], w_proj_f[i], w_out[i], ln_g[i], ln_b[i])
    return h[:, N_META:]


def setup_inputs(seed: int = 0) -> dict:
    key = jax.random.key(seed)
    ks = jax.random.split(key, 18)
    nrm = lambda k, s, sc: jax.random.normal(k, s, jnp.float32) * sc
    dt = jnp.exp(jax.random.uniform(ks[8], (DEPTH, 2, DN_HEADS), jnp.float32,
                                    math.log(1e-3), math.log(1e-1)))
    return {
        "x_prompt": nrm(ks[0], (BATCH, SEQ, D_MODEL), 1.0),
        "x_sample": nrm(ks[1], (DEC_BATCH, DEC_SEQ, D_MODEL), 1.0),
        "meta_tokens": nrm(ks[2], (N_META, D_MODEL), 1.0),
        "ln_in_g": 1.0 + nrm(ks[3], (D_MODEL,), 0.02),
        "ln_in_b": nrm(ks[4], (D_MODEL,), 0.02),
        "w_in": nrm(ks[5], (DEPTH, D_MODEL, D_IN), D_MODEL ** -0.5),
        "b_gate": nrm(ks[6], (DEPTH, N_BRANCH * D_MODEL), 0.02),
        "conv_w": nrm(ks[7], (DEPTH, CONV_K, 3 * DN_WIDTH), CONV_K ** -0.5),
        "a_log": jnp.log(jax.random.uniform(ks[9], (DEPTH, 2, DN_HEADS), jnp.float32, 1.0, 16.0)),
        "dt_bias": dt + jnp.log(-jnp.expm1(-dt)),
        "dn_norm_w": 1.0 + nrm(ks[10], (DEPTH, DN_HEAD_DIM), 0.02),
        "w_proj_a": nrm(ks[11], (DEPTH, DN_WIDTH, D_MODEL), DN_WIDTH ** -0.5 * DEEPNORM_BETA),
        "w_proj_f": nrm(ks[12], (DEPTH, FN_WIDTH, D_MODEL), FN_WIDTH ** -0.5 * DEEPNORM_BETA),
        "w_out": nrm(ks[13], (DEPTH, D_MODEL, D_MODEL), D_MODEL ** -0.5 * DEEPNORM_BETA),
        "ln_g": 1.0 + nrm(ks[14], (DEPTH, D_MODEL), 0.02),
        "ln_b": nrm(ks[15], (DEPTH, D_MODEL), 0.02),
    }


def reference(x_prompt, x_sample, meta_tokens, ln_in_g, ln_in_b, w_in, b_gate, conv_w, a_log, dt_bias,
              dn_norm_w, w_proj_a, w_proj_f, w_out, ln_g, ln_b):
    y_prompt = encode(x_prompt, meta_tokens, ln_in_g, ln_in_b, w_in, b_gate, conv_w, a_log, dt_bias,
                      dn_norm_w, w_proj_a, w_proj_f, w_out, ln_g, ln_b)
    y_sample = encode(x_sample, meta_tokens, ln_in_g, ln_in_b, w_in, b_gate, conv_w, a_log, dt_bias,
                      dn_norm_w, w_proj_a, w_proj_f, w_out, ln_g, ln_b)
    return (y_prompt, y_sample)
```

```python
import functools
import math

import numpy as np
import jax
import jax.numpy as jnp
from jax import lax
from jax.experimental import pallas as pl
from jax.experimental.pallas import tpu as pltpu

F32 = jnp.float32
BF16 = jnp.bfloat16

D_MODEL = 2048
N_META = 16
HEADS = 8
HEAD_DIM = 128
DN_WIDTH = HEADS * HEAD_DIM
CONV_K = 5
FN_GROUPS = 4
FN_GROUP_DIM = 256
FN_WIDTH = FN_GROUPS * FN_GROUP_DIM
DEPTH = 1
DEEPNORM_ALPHA = (2 * DEPTH) ** 0.25
LN_EPS = 1e-5
RMS_EPS = 1e-6
L2_EPS = 1e-6

CHUNK = 64
DELTA_BLOCK = 256
META_BLOCK = 128
LANE = 128
VMEM_LIMIT = 56 * 1024 * 1024

PCOL_Q, PCOL_K, PCOL_V, PCOL_ZA, PCOL_F, PCOL_ZF = range(6)
P_WIDTH = 6 * 1024
BG_BETA, BG_G, BG_GC, BG_TOT = 0, 16, 32, 48


def _mm(a, b):
    return jnp.dot(a.astype(BF16), b.astype(BF16), preferred_element_type=F32)


def _mm_nt(a, b):
    return lax.dot_general(a.astype(BF16), b.astype(BF16), (((1,), (1,)), ((), ())),
                           preferred_element_type=F32)


def _mm_tn(a, b):
    return lax.dot_general(a.astype(BF16), b.astype(BF16), (((0,), (0,)), ((), ())),
                           preferred_element_type=F32)


def _layer_norm(x, g, b):
    mu = jnp.mean(x, axis=-1, keepdims=True)
    xc = x - mu
    var = jnp.mean(xc * xc, axis=-1, keepdims=True)
    return xc * lax.rsqrt(var + LN_EPS) * g + b


def _silu(x):
    return x * jax.nn.sigmoid(x)


def _row_tile(rows, cap, mult=16):
    best = None
    for t in range(mult, min(rows, cap) + 1, mult):
        if rows % t == 0:
            best = t
    assert best is not None, (rows, cap, mult)
    return best


def _ln_inproj_kernel(x_ref, g_ref, b_ref, w_ref, wba_ref, p_ref, ba_ref, h_ref):
    @pl.when(pl.program_id(1) == 0)
    def _():
        hb = _layer_norm(x_ref[...], g_ref[...], b_ref[...]).astype(BF16)
        h_ref[...] = hb
        ba_ref[...] = jnp.dot(hb, wba_ref[...], preferred_element_type=F32)

    p_ref[...] = jnp.dot(h_ref[...], w_ref[...], preferred_element_type=F32).astype(p_ref.dtype)


def _ln_inproj(x, ln_g, ln_b, w_p, w_ba, *, tm, tn=1024):
    rows = x.shape[0]
    assert rows % tm == 0 and P_WIDTH % tn == 0
    return pl.pallas_call(
        _ln_inproj_kernel,
        grid=(rows // tm, P_WIDTH // tn),
        in_specs=[
            pl.BlockSpec((tm, D_MODEL), lambda i, j: (i, 0)),
            pl.BlockSpec((1, D_MODEL), lambda i, j: (0, 0)),
            pl.BlockSpec((1, D_MODEL), lambda i, j: (0, 0)),
            pl.BlockSpec((D_MODEL, tn), lambda i, j: (0, j)),
            pl.BlockSpec((D_MODEL, LANE), lambda i, j: (0, 0)),
        ],
        out_specs=[
            pl.BlockSpec((tm, tn), lambda i, j: (i, j)),
            pl.BlockSpec((tm, LANE), lambda i, j: (i, 0)),
        ],
        out_shape=[
            jax.ShapeDtypeStruct((rows, P_WIDTH), BF16),
            jax.ShapeDtypeStruct((rows, LANE), F32),
        ],
        scratch_shapes=[pltpu.VMEM((tm, D_MODEL), BF16)],
        compiler_params=pltpu.CompilerParams(
            dimension_semantics=("parallel", "arbitrary"), vmem_limit_bytes=VMEM_LIMIT),
        name="ln_inproj",
    )(x, ln_g, ln_b, w_p, w_ba)


def _split3(x):
    hi = x.astype(BF16)
    r1 = x - hi.astype(F32)
    mid = r1.astype(BF16)
    lo = (r1 - mid.astype(F32)).astype(BF16)
    return hi, mid, lo


def _prep_kernel(p_ref, halo_ref, ba_ref, cw_ref, gpar_ref, qkv_ref, bgc_ref, gct_ref, *, tm, valid_rows):
    x = p_ref[...].astype(F32)
    halo = halo_ref[...]
    cw = cw_ref[...]
    row = lax.broadcasted_iota(jnp.int32, (tm, 1), 0)

    def shifted(s):
        y = pltpu.roll(x, (-s) % tm, axis=0)
        if s < 0:
            for t in range(-s):
                y = jnp.where(row == t, halo[2 + s + t:3 + s + t, :], y)
        else:
            for t in range(s):
                y = jnp.where(row == tm - s + t, halo[2 + t:3 + t, :], y)
        return y

    acc = x * cw[2:3, :]
    for s in (-2, -1, 1, 2):
        acc = acc + shifted(s) * cw[2 + s:3 + s, :]
    act = _silu(acc)

    if valid_rows is not None:
        keep = row < valid_rows
        act = jnp.where(keep, act, 0.0)

    for h in range(HEADS):
        for base, scale in ((0, HEAD_DIM ** -0.5), (DN_WIDTH, 1.0)):
            sl = slice(base + h * HEAD_DIM, base + (h + 1) * HEAD_DIM)
            t = act[:, sl]
            n = lax.rsqrt(jnp.sum(t * t, axis=-1, keepdims=True) + L2_EPS)
            qkv_ref[:, sl] = (t * (n * scale)).astype(qkv_ref.dtype)
    qkv_ref[:, 2 * DN_WIDTH:] = act[:, 2 * DN_WIDTH:].astype(qkv_ref.dtype)

    ba = ba_ref[...]
    gpar = gpar_ref[...]
    lane = lax.broadcasted_iota(jnp.int32, (tm, LANE), 1)
    beta = jax.nn.sigmoid(ba)
    z = ba + gpar[1:2, :]
    softplus = jnp.maximum(z, 0.0) + jnp.log1p(jnp.exp(-jnp.abs(z)))
    g = -jnp.exp(gpar[0:1, :]) * softplus
    is_g = (lane >= BG_G) & (lane < BG_GC)
    g = jnp.where(is_g, g, 0.0)
    beta = jnp.where(lane < BG_G, beta, 0.0)
    if valid_rows is not None:
        g = jnp.where(keep, g, 0.0)
        beta = jnp.where(keep, beta, 0.0)

    r = lax.broadcasted_iota(jnp.int32, (tm, tm), 0)
    c = lax.broadcasted_iota(jnp.int32, (tm, tm), 1)
    same = (r // CHUNK) == (c // CHUNK)
    m_pre = jnp.where(same & (r >= c), 1.0, 0.0).astype(BF16)
    m_suf = jnp.where(same & (r <= c), 1.0, 0.0).astype(BF16)
    pre = jnp.zeros((tm, LANE), F32)
    suf = jnp.zeros((tm, LANE), F32)
    for part in _split3(g):
        pre = pre + jnp.dot(m_pre, part, preferred_element_type=F32)
        suf = suf + jnp.dot(m_suf, part, preferred_element_type=F32)
    tot = pre + suf - g
    fwd_lane = lane < BG_G + HEADS
    gc = jnp.where(fwd_lane, pre, suf)
    out = beta + g + pltpu.roll(gc, BG_GC - BG_G, axis=1) + pltpu.roll(tot, BG_TOT - BG_G, axis=1)
    bgc_ref[...] = out
    gct_ref[...] = out.T[BG_GC:BG_GC + 2 * HEADS, :]


def _prep(p, halo, ba, conv_w8, gpar, *, batch, nt, tm, valid_rows=None, p_batched=True):
    rows = batch * nt * tm
    if p_batched:
        pmap = lambda b, i: (b * nt + i, 0)
    else:
        pmap = lambda b, i: (i, 0)
    omap = lambda b, i: (b * nt + i, 0)
    return pl.pallas_call(
        functools.partial(_prep_kernel, tm=tm, valid_rows=valid_rows),
        grid=(batch, nt),
        in_specs=[
            pl.BlockSpec((tm, 3 * DN_WIDTH), pmap),
            pl.BlockSpec((None, None, 8, 3 * DN_WIDTH), lambda b, i: (b, i, 0, 0)),
            pl.BlockSpec((tm, LANE), pmap),
            pl.BlockSpec((8, 3 * DN_WIDTH), lambda b, i: (0, 0)),
            pl.BlockSpec((8, LANE), lambda b, i: (0, 0)),
        ],
        out_specs=[
            pl.BlockSpec((tm, 3 * DN_WIDTH), omap),
            pl.BlockSpec((tm, LANE), omap),
            pl.BlockSpec((2 * HEADS, tm), lambda b, i: (0, b * nt + i)),
        ],
        out_shape=[
            jax.ShapeDtypeStruct((rows, 3 * DN_WIDTH), BF16),
            jax.ShapeDtypeStruct((rows, LANE), F32),
            jax.ShapeDtypeStruct((2 * HEADS, rows), F32),
        ],
        compiler_params=pltpu.CompilerParams(
            dimension_semantics=("parallel", "parallel"), vmem_limit_bytes=VMEM_LIMIT),
        name="prep",
    )(p, halo, ba, conv_w8, gpar)


def _unit_lower_inverse(lm, eye):
    x = -lm
    t = eye + x
    p = _mm(x, x)
    steps = int(math.log2(CHUNK)) - 1
    for s in range(steps):
        t = t + _mm(t, p)
        if s + 1 < steps:
            p = _mm(p, p)
    return t


def _delta_kernel(q_ref, k_ref, v_ref, bgc_ref, gct_ref, s0_ref, o_ref, sfin_ref, s_scr, *, reverse, nchunk):
    i = pl.program_id(1)

    @pl.when(i == 0)
    def _():
        s_scr[...] = s0_ref[...]

    dcol = HEADS if reverse else 0
    row = lax.broadcasted_iota(jnp.int32, (CHUNK, CHUNK), 0)
    col = lax.broadcasted_iota(jnp.int32, (CHUNK, CHUNK), 1)
    incl = (row <= col) if reverse else (row >= col)
    strict = (row < col) if reverse else (row > col)
    eye = jnp.where(row == col, 1.0, 0.0).astype(F32)

    chunks = range(nchunk - 1, -1, -1) if reverse else range(nchunk)
    for c in chunks:
        rs = slice(c * CHUNK, (c + 1) * CHUNK)
        bg = bgc_ref[rs, :]
        g_rows = gct_ref[:, rs]
        for h in range(HEADS):
            j = dcol + h
            hs = slice(h * HEAD_DIM, (h + 1) * HEAD_DIM)
            beta = bg[:, BG_BETA + j:BG_BETA + j + 1]
            gc = bg[:, BG_GC + j:BG_GC + j + 1]
            tot = bg[:, BG_TOT + j:BG_TOT + j + 1]
            gr = g_rows[j:j + 1, :]
            kb = k_ref[rs, hs]
            qb = q_ref[rs, hs]
            kf = kb.astype(F32)
            egc = jnp.exp(gc)
            decay = jnp.exp(jnp.where(incl, gc - gr, -1e30))
            lm = jnp.where(strict, beta * _mm_nt(kb, kb) * decay, 0.0)
            pm = _mm_nt(qb, kb) * decay
            tm_inv = _unit_lower_inverse(lm, eye)
            pt = _mm(pm, tm_inv)
            ktt = _mm_tn(jnp.exp(tot - gc) * kf, tm_inv)
            kw = (beta * egc) * kf
            qd = egc * qb.astype(F32)
            bv = beta * v_ref[rs, hs].astype(F32)

            s = s_scr[h]
            xs = _mm(jnp.concatenate([kw, qd], axis=0), s)
            r = bv - xs[:CHUNK]
            o = xs[CHUNK:] + _mm(pt, r)
            s_scr[h] = jnp.exp(tot[0:1, :]) * s + _mm(ktt, r)
            o_ref[rs, hs] = o.astype(o_ref.dtype)

    @pl.when(i == pl.num_programs(1) - 1)
    def _():
        sfin_ref[...] = s_scr[...]


def _delta(qkv, bgc, gct, s0, *, batch, nblk, tb, reverse):
    rows = batch * nblk * tb
    assert tb % CHUNK == 0
    if reverse:
        blk = lambda b, i: b * nblk + (nblk - 1 - i)
    else:
        blk = lambda b, i: b * nblk + i
    qspec = lambda col: pl.BlockSpec((tb, DN_WIDTH), lambda b, i: (blk(b, i), col))
    sspec = pl.BlockSpec((None, HEADS, HEAD_DIM, HEAD_DIM), lambda b, i: (b, 0, 0, 0))
    return pl.pallas_call(
        functools.partial(_delta_kernel, reverse=reverse, nchunk=tb // CHUNK),
        grid=(batch, nblk),
        in_specs=[
            qspec(0), qspec(1), qspec(2),
            pl.BlockSpec((tb, LANE), lambda b, i: (blk(b, i), 0)),
            pl.BlockSpec((2 * HEADS, tb), lambda b, i: (0, blk(b, i))),
            sspec,
        ],
        out_specs=[
            pl.BlockSpec((tb, DN_WIDTH), lambda b, i: (blk(b, i), 0)),
            sspec,
        ],
        out_shape=[
            jax.ShapeDtypeStruct((rows, DN_WIDTH), BF16),
            jax.ShapeDtypeStruct((batch, HEADS, HEAD_DIM, HEAD_DIM), F32),
        ],
        scratch_shapes=[pltpu.VMEM((HEADS, HEAD_DIM, HEAD_DIM), F32)],
        compiler_params=pltpu.CompilerParams(
            dimension_semantics=("parallel", "arbitrary"), vmem_limit_bytes=VMEM_LIMIT),
        name="delta_bwd" if reverse else "delta_fwd",
    )(qkv, qkv, qkv, bgc, gct, s0)


def _dft_tables(n):
    k = np.arange(n, dtype=np.int64)
    ang = 2.0 * np.pi * ((k[:, None] * k[None, :]) % n).astype(np.float64) / n
    return np.cos(ang), np.sin(ang)


def _fft_factor(t):
    if t <= 4096:
        return t, 1
    best = None
    for n1 in range(16, t, 16):
        if t % n1:
            continue
        n2 = t // n1
        n2p = -(-n2 // 16) * 16
        cost = 4 * n1 + 2 * n2p
        if best is None or cost < best[0]:
            best = (cost, n1, n2)
    assert best is not None, t
    return best[1], best[2]


def _caxis_kernel(f_ref, cs_ref, a_ref, b_ref):
    cs = cs_ref[...]
    for g in range(FN_GROUPS):
        sl = slice(g * FN_GROUP_DIM, (g + 1) * FN_GROUP_DIM)
        ab = jnp.dot(f_ref[:, sl], cs, preferred_element_type=F32)
        a_ref[:, sl] = ab[:, :FN_GROUP_DIM].astype(a_ref.dtype)
        b_ref[:, sl] = ab[:, FN_GROUP_DIM:].astype(b_ref.dtype)


def _caxis(f, cs):
    rows = f.shape[0]
    tr = _row_tile(rows, 1024)
    spec = pl.BlockSpec((tr, FN_WIDTH), lambda i: (i, 0))
    return pl.pallas_call(
        _caxis_kernel,
        grid=(rows // tr,),
        in_specs=[spec, pl.BlockSpec((FN_GROUP_DIM, 2 * FN_GROUP_DIM), lambda i: (0, 0))],
        out_specs=[spec, spec],
        out_shape=[jax.ShapeDtypeStruct((rows, FN_WIDTH), BF16)] * 2,
        compiler_params=pltpu.CompilerParams(
            dimension_semantics=("parallel",), vmem_limit_bytes=VMEM_LIMIT),
        name="dft_channel",
    )(f, cs)


def _dft1_dense_kernel(a_ref, b_ref, c_ref, s_ref, o_ref, *, scale):
    y = (jnp.dot(c_ref[...], a_ref[...], preferred_element_type=F32)
         - jnp.dot(s_ref[...], b_ref[...], preferred_element_type=F32))
    o_ref[...] = (y * scale).astype(o_ref.dtype)


def _dft1_kernel(a_ref, b_ref, c_ref, s_ref, twr_ref, twi_ref, yre_ref, yim_ref, *, n2):
    t2 = pl.program_id(1)

    @pl.when(t2 < n2)
    def _():
        a = a_ref[...]
        b = b_ref[...]
        c = c_ref[...]
        s = s_ref[...]
        yre = jnp.dot(c, a, preferred_element_type=F32) - jnp.dot(s, b, preferred_element_type=F32)
        yim = -(jnp.dot(c, b, preferred_element_type=F32) + jnp.dot(s, a, preferred_element_type=F32))
        wr = twr_ref[...]
        wi = twi_ref[...]
        for l in range(FN_WIDTH // LANE):
            sl = slice(l * LANE, (l + 1) * LANE)
            yre_ref[:, sl] = (yre[:, sl] * wr - yim[:, sl] * wi).astype(yre_ref.dtype)
            yim_ref[:, sl] = (yre[:, sl] * wi + yim[:, sl] * wr).astype(yim_ref.dtype)

    @pl.when(t2 >= n2)
    def _():
        yre_ref[...] = jnp.zeros_like(yre_ref)
        yim_ref[...] = jnp.zeros_like(yim_ref)


def _dft2_kernel(yre_ref, yim_ref, c_ref, s_ref, o_ref, *, n2, scale):
    y = (jnp.dot(c_ref[...], yre_ref[...], preferred_element_type=F32)
         + jnp.dot(s_ref[...], yim_ref[...], preferred_element_type=F32))
    o_ref[...] = (y[:n2] * scale).astype(o_ref.dtype)


def _fourier_time(a, b, *, batch, t, factor=None):
    n1, n2 = factor if factor is not None else _fft_factor(t)
    assert n1 * n2 == t
    scale = 1.0 / math.sqrt(t * FN_GROUP_DIM)
    c1, s1 = _dft_tables(n1)
    c1 = jnp.asarray(c1, BF16)
    s1 = jnp.asarray(s1, BF16)
    if n2 == 1:
        tmi = _row_tile(n1, 1024)
        return pl.pallas_call(
            functools.partial(_dft1_dense_kernel, scale=scale),
            grid=(batch, n1 // tmi),
            in_specs=[
                pl.BlockSpec((None, n1, FN_WIDTH), lambda bb, m: (bb, 0, 0)),
                pl.BlockSpec((None, n1, FN_WIDTH), lambda bb, m: (bb, 0, 0)),
                pl.BlockSpec((tmi, n1), lambda bb, m: (m, 0)),
                pl.BlockSpec((tmi, n1), lambda bb, m: (m, 0)),
            ],
            out_specs=pl.BlockSpec((None, tmi, FN_WIDTH), lambda bb, m: (bb, m, 0)),
            out_shape=jax.ShapeDtypeStruct((batch, t, FN_WIDTH), BF16),
            compiler_params=pltpu.CompilerParams(
                dimension_semantics=("parallel", "parallel"), vmem_limit_bytes=VMEM_LIMIT),
            name="dft_time_dense",
        )(a, b, c1, s1)

    n2p = -(-n2 // 16) * 16
    k1 = np.arange(n1, dtype=np.int64)
    t2 = np.arange(n2p, dtype=np.int64)
    ang = 2.0 * np.pi * ((t2[:, None] * k1[None, :]) % t).astype(np.float64) / t
    twr = jnp.asarray(np.broadcast_to(np.cos(ang)[:, :, None], (n2p, n1, LANE)), F32)
    twi = jnp.asarray(np.broadcast_to(-np.sin(ang)[:, :, None], (n2p, n1, LANE)), F32)
    a2 = a.reshape(batch, n1, n2 * FN_WIDTH)
    b2 = b.reshape(batch, n1, n2 * FN_WIDTH)
    in_col = lambda bb, j: (bb, 0, jnp.minimum(j, n2 - 1))
    yre, yim = pl.pallas_call(
        functools.partial(_dft1_kernel, n2=n2),
        grid=(batch, n2p),
        in_specs=[
            pl.BlockSpec((None, n1, FN_WIDTH), in_col),
            pl.BlockSpec((None, n1, FN_WIDTH), in_col),
            pl.BlockSpec((n1, n1), lambda bb, j: (0, 0)),
            pl.BlockSpec((n1, n1), lambda bb, j: (0, 0)),
            pl.BlockSpec((None, n1, LANE), lambda bb, j: (j, 0, 0)),
            pl.BlockSpec((None, n1, LANE), lambda bb, j: (j, 0, 0)),
        ],
        out_specs=[pl.BlockSpec((None, None, n1, FN_WIDTH), lambda bb, j: (bb, j, 0, 0))] * 2,
        out_shape=[jax.ShapeDtypeStruct((batch, n2p, n1, FN_WIDTH), BF16)] * 2,
        compiler_params=pltpu.CompilerParams(
            dimension_semantics=("parallel", "parallel"), vmem_limit_bytes=VMEM_LIMIT),
        name="dft_time_1",
    )(a2, b2, c1, s1, twr, twi)

    c2 = np.zeros((n2p, n2p))
    s2 = np.zeros((n2p, n2p))
    c2[:n2, :n2], s2[:n2, :n2] = _dft_tables(n2)
    width = n1 * FN_WIDTH
    tn = 2048
    assert width % tn == 0
    out = pl.pallas_call(
        functools.partial(_dft2_kernel, n2=n2, scale=scale),
        grid=(batch, width // tn),
        in_specs=[
            pl.BlockSpec((None, n2p, tn), lambda bb, j: (bb, 0, j)),
            pl.BlockSpec((None, n2p, tn), lambda bb, j: (bb, 0, j)),
            pl.BlockSpec((n2p, n2p), lambda bb, j: (0, 0)),
            pl.BlockSpec((n2p, n2p), lambda bb, j: (0, 0)),
        ],
        out_specs=pl.BlockSpec((None, n2, tn), lambda bb, j: (bb, 0, j)),
        out_shape=jax.ShapeDtypeStruct((batch, n2, width), BF16),
        compiler_params=pltpu.CompilerParams(
            dimension_semantics=("parallel", "parallel"), vmem_limit_bytes=VMEM_LIMIT),
        name="dft_time_2",
    )(yre.reshape(batch, n2p, width), yim.reshape(batch, n2p, width),
      jnp.asarray(c2, BF16), jnp.asarray(s2, BF16))
    return out.reshape(batch, t, FN_WIDTH)


def _merge_kernel(x_ref, of_ref, ob_ref, za_ref, zf_ref, fr_ref, lig_ref, lib_ref, nw_ref,
                  wga_ref, wgf_ref, bga_ref, bgf_ref, wa_ref, wf_ref, wo_ref, lg_ref, lb_ref,
                  y_ref, hb_ref, ya_ref, yf_ref, acc_ref):
    j = pl.program_id(1)

    @pl.when(j == 0)
    def _():
        hb_ref[...] = _layer_norm(x_ref[...], lig_ref[...], lib_ref[...]).astype(BF16)
        nw = nw_ref[...]
        for h in range(HEADS):
            hs = slice(h * HEAD_DIM, (h + 1) * HEAD_DIM)
            o = of_ref[:, hs].astype(F32) + ob_ref[:, hs].astype(F32)
            o = o * lax.rsqrt(jnp.mean(o * o, axis=-1, keepdims=True) + RMS_EPS) * nw
            ya_ref[:, hs] = (o * _silu(za_ref[:, hs].astype(F32))).astype(BF16)
        yf_ref[...] = (fr_ref[...].astype(F32) * _silu(zf_ref[...].astype(F32))).astype(BF16)
        acc_ref[...] = jnp.zeros_like(acc_ref)

    hb = hb_ref[...]
    ga = jax.nn.sigmoid(jnp.dot(hb, wga_ref[...], preferred_element_type=F32) + bga_ref[...])
    gf = jax.nn.sigmoid(jnp.dot(hb, wgf_ref[...], preferred_element_type=F32) + bgf_ref[...])
    m = (ga * jnp.dot(ya_ref[...], wa_ref[...], preferred_element_type=F32)
         + gf * jnp.dot(yf_ref[...], wf_ref[...], preferred_element_type=F32))
    acc_ref[...] += jnp.dot(m.astype(BF16), wo_ref[...], preferred_element_type=F32)

    @pl.when(j == pl.num_programs(1) - 1)
    def _():
        h = _layer_norm(x_ref[...], lig_ref[...], lib_ref[...])
        y_ref[...] = _layer_norm(DEEPNORM_ALPHA * h + acc_ref[...], lg_ref[...], lb_ref[...])


def _merge(x, o_f, o_b, p, fr, ln_in_g, ln_in_b, nw, w_g, b_g, w_a, w_f, w_o, ln_g, ln_b, *, tm=512, tn=256):
    rows = x.shape[0]
    assert rows % tm == 0 and D_MODEL % tn == 0
    nj = D_MODEL // tn
    row_spec = lambda w, col=0: pl.BlockSpec((tm, w), lambda i, j: (i, col))
    vec_spec = lambda w: pl.BlockSpec((1, w), lambda i, j: (0, 0))
    return pl.pallas_call(
        _merge_kernel,
        grid=(rows // tm, nj),
        in_specs=[
            row_spec(D_MODEL),
            row_spec(DN_WIDTH), row_spec(DN_WIDTH),
            row_spec(DN_WIDTH, PCOL_ZA), row_spec(FN_WIDTH, PCOL_ZF),
            row_spec(FN_WIDTH),
            vec_spec(D_MODEL), vec_spec(D_MODEL), vec_spec(HEAD_DIM),
            pl.BlockSpec((D_MODEL, tn), lambda i, j: (0, j)),
            pl.BlockSpec((D_MODEL, tn), lambda i, j: (0, nj + j)),
            pl.BlockSpec((1, tn), lambda i, j: (0, j)),
            pl.BlockSpec((1, tn), lambda i, j: (0, nj + j)),
            pl.BlockSpec((DN_WIDTH, tn), lambda i, j: (0, j)),
            pl.BlockSpec((FN_WIDTH, tn), lambda i, j: (0, j)),
            pl.BlockSpec((tn, D_MODEL), lambda i, j: (j, 0)),
            vec_spec(D_MODEL), vec_spec(D_MODEL),
        ],
        out_specs=pl.BlockSpec((tm, D_MODEL), lambda i, j: (i, 0)),
        out_shape=jax.ShapeDtypeStruct((rows, D_MODEL), F32),
        scratch_shapes=[
            pltpu.VMEM((tm, D_MODEL), BF16),
            pltpu.VMEM((tm, DN_WIDTH), BF16),
            pltpu.VMEM((tm, FN_WIDTH), BF16),
            pltpu.VMEM((tm, D_MODEL), F32),
        ],
        compiler_params=pltpu.CompilerParams(
            dimension_semantics=("parallel", "arbitrary"), vmem_limit_bytes=VMEM_LIMIT),
        name="merge",
    )(x, o_f, o_b, p, p, fr, ln_in_g, ln_in_b, nw, w_g, w_g, b_g, b_g, w_a, w_f, w_o, ln_g, ln_b)


def _encode(x, p_meta, ba_meta, wts, *, fft_factor=None, tm_rows=1024):
    batch, seq, _ = x.shape
    rows = batch * seq
    t_full = N_META + seq
    x2 = x.reshape(rows, D_MODEL)

    p, ba = _ln_inproj(x2, wts["ln_in_g"], wts["ln_in_b"], wts["w_p"], wts["w_ba"],
                       tm=_row_tile(rows, tm_rows))

    tb = DELTA_BLOCK
    assert seq % tb == 0 and seq >= META_BLOCK
    nt = seq // tb
    p4 = p.reshape(batch, nt, tb, P_WIDTH)
    tail = p4[:, :, tb - 2:, :3 * DN_WIDTH].astype(F32)
    head = p4[:, :, :2, :3 * DN_WIDTH].astype(F32)
    meta_tail = jnp.broadcast_to(p_meta[None, None, N_META - 2:, :3 * DN_WIDTH].astype(F32),
                                 (batch, 1, 2, 3 * DN_WIDTH))
    prev = jnp.concatenate([meta_tail, tail[:, :-1]], axis=1)
    nxt = jnp.concatenate([head[:, 1:], jnp.zeros((batch, 1, 2, 3 * DN_WIDTH), F32)], axis=1)
    halo = jnp.concatenate([prev, nxt, jnp.zeros((batch, nt, 4, 3 * DN_WIDTH), F32)], axis=2)
    qkv, bgc, gct = _prep(p, halo, ba, wts["conv_w8"], wts["gpar"], batch=batch, nt=nt, tm=tb)

    mb = META_BLOCK
    p_m = jnp.concatenate([
        jnp.broadcast_to(p_meta[None, :, :3 * DN_WIDTH], (batch, N_META, 3 * DN_WIDTH)),
        p.reshape(batch, seq, P_WIDTH)[:, :mb - N_META, :3 * DN_WIDTH]], axis=1).reshape(batch * mb, 3 * DN_WIDTH)
    ba_m = jnp.concatenate([
        jnp.broadcast_to(ba_meta[None], (batch, N_META, LANE)),
        ba.reshape(batch, seq, LANE)[:, :mb - N_META]], axis=1).reshape(batch * mb, LANE)
    halo_m = jnp.zeros((batch, 1, 8, 3 * DN_WIDTH), F32)
    qkv_m, bgc_m, gct_m = _prep(p_m, halo_m, ba_m, wts["conv_w8"], wts["gpar"],
                                batch=batch, nt=1, tm=mb, valid_rows=N_META)
    s_zero = jnp.zeros((batch, HEADS, HEAD_DIM, HEAD_DIM), F32)
    _, s_meta = _delta(qkv_m, bgc_m, gct_m, s_zero, batch=batch, nblk=1, tb=mb, reverse=False)

    o_f, _ = _delta(qkv, bgc, gct, s_meta, batch=batch, nblk=nt, tb=tb, reverse=False)
    o_b, _ = _delta(qkv, bgc, gct, s_zero, batch=batch, nblk=nt, tb=tb, reverse=True)

    f_x = p.reshape(batch, seq, P_WIDTH)[:, :, PCOL_F * 1024:(PCOL_F + 1) * 1024]
    f_m = jnp.broadcast_to(p_meta[None, :, PCOL_F * 1024:(PCOL_F + 1) * 1024], (batch, N_META, FN_WIDTH))
    f_full = jnp.concatenate([f_m, f_x], axis=1).reshape(batch * t_full, FN_WIDTH)
    a, b = _caxis(f_full, wts["cs"])
    fr = _fourier_time(a.reshape(batch, t_full, FN_WIDTH), b.reshape(batch, t_full, FN_WIDTH),
                       batch=batch, t=t_full, factor=fft_factor)
    fr = fr[:, N_META:].reshape(rows, FN_WIDTH)

    y = _merge(x2, o_f, o_b, p, fr, wts["ln_in_g"], wts["ln_in_b"], wts["nw"], wts["w_g"], wts["b_g"],
               wts["w_a"], wts["w_f"], wts["w_o"], wts["ln_g"], wts["ln_b"], tm=_row_tile(rows, 512))
    return y.reshape(batch, seq, D_MODEL)


def _prepare_weights(ln_in_g, ln_in_b, w_in, b_gate, conv_w, a_log, dt_bias, dn_norm_w,
                     w_proj_a, w_proj_f, w_out, ln_g, ln_b):
    w = w_in[0]
    col_b = 4 * DN_WIDTH
    col_f = col_b + 4 * HEADS
    col_g = col_f + 2 * FN_WIDTH
    w_p = jnp.concatenate([w[:, :col_b], w[:, col_f:col_g]], axis=1).astype(BF16)
    w_ba = jnp.pad(w[:, col_b:col_f], ((0, 0), (0, LANE - 4 * HEADS))).astype(BF16)
    gpar = jnp.zeros((8, LANE), F32)
    gpar = gpar.at[0, BG_G:BG_GC].set(a_log[0].reshape(-1).astype(F32))
    gpar = gpar.at[1, BG_G:BG_GC].set(dt_bias[0].reshape(-1).astype(F32))
    k = np.arange(FN_GROUP_DIM, dtype=np.int64)
    ang = 2.0 * np.pi * ((k[:, None] * k[None, :]) % FN_GROUP_DIM) / FN_GROUP_DIM
    cs = jnp.asarray(np.concatenate([np.cos(ang), np.sin(ang)], axis=1), BF16)
    return {
        "ln_in_g": ln_in_g.reshape(1, D_MODEL).astype(F32),
        "ln_in_b": ln_in_b.reshape(1, D_MODEL).astype(F32),
        "w_p": w_p,
        "w_ba": w_ba,
        "conv_w8": jnp.pad(conv_w[0].astype(F32), ((0, 8 - CONV_K), (0, 0))),
        "gpar": gpar,
        "cs": cs,
        "nw": dn_norm_w[0].reshape(1, HEAD_DIM).astype(F32),
        "w_g": w[:, col_g:].astype(BF16),
        "b_g": b_gate[0].reshape(1, 2 * D_MODEL).astype(F32),
        "w_a": w_proj_a[0].astype(BF16),
        "w_f": w_proj_f[0].astype(BF16),
        "w_o": w_out[0].astype(BF16),
        "ln_g": ln_g[0].reshape(1, D_MODEL).astype(F32),
        "ln_b": ln_b[0].reshape(1, D_MODEL).astype(F32),
    }


def kernel(x_prompt, x_sample, meta_tokens, ln_in_g, ln_in_b, w_in, b_gate, conv_w, a_log, dt_bias, dn_norm_w,
           w_proj_a, w_proj_f, w_out, ln_g, ln_b):
    wts = _prepare_weights(ln_in_g, ln_in_b, w_in, b_gate, conv_w, a_log, dt_bias, dn_norm_w,
                           w_proj_a, w_proj_f, w_out, ln_g, ln_b)
    p_meta, ba_meta = _ln_inproj(meta_tokens.astype(F32), wts["ln_in_g"], wts["ln_in_b"],
                                 wts["w_p"], wts["w_ba"], tm=N_META)
    y_prompt = _encode(x_prompt, p_meta, ba_meta, wts)
    y_sample = _encode(x_sample, p_meta, ba_meta, wts)
    return (y_prompt, y_sample)
```

```python
import functools
import math

import numpy as np
import jax
import jax.numpy as jnp
from jax import lax
from jax.experimental import pallas as pl
from jax.experimental.pallas import tpu as pltpu

F32 = jnp.float32
BF16 = jnp.bfloat16

D_MODEL = 2048
N_META = 16
HEADS = 8
HEAD_DIM = 128
DN_WIDTH = HEADS * HEAD_DIM
CONV_K = 5
FN_GROUPS = 4
FN_GROUP_DIM = 256
FN_WIDTH = FN_GROUPS * FN_GROUP_DIM
DEPTH = 1
DEEPNORM_ALPHA = (2 * DEPTH) ** 0.25
LN_EPS = 1e-5
RMS_EPS = 1e-6
L2_EPS = 1e-6

LANE = 128
CHUNK = HEAD_DIM
PAIR = 2 * HEAD_DIM
NPAIR = HEADS // 2
INV_BASE = 16
DELTA_BLOCK = 256
META_BLOCK = CHUNK
VMEM_LIMIT = 56 * 1024 * 1024

PCOL_Q, PCOL_K, PCOL_V, PCOL_ZA, PCOL_F, PCOL_ZF = range(6)
P_WIDTH = 6 * 1024
BG_BETA, BG_G, BG_GC, BG_TOT = 0, 16, 32, 48
GT_ROWS = 64


def _mm(a, b):
    return jnp.dot(a.astype(BF16), b.astype(BF16), preferred_element_type=F32)


def _layer_norm(x, g, b):
    mu = jnp.mean(x, axis=-1, keepdims=True)
    xc = x - mu
    var = jnp.mean(xc * xc, axis=-1, keepdims=True)
    return xc * lax.rsqrt(var + LN_EPS) * g + b


def _silu(x):
    return x * jax.nn.sigmoid(x)


def _row_tile(rows, cap, mult=16):
    best = None
    for t in range(mult, min(rows, cap) + 1, mult):
        if rows % t == 0:
            best = t
    assert best is not None, (rows, cap, mult)
    return best


def _ln_inproj_kernel(x_ref, g_ref, b_ref, w_ref, wba_ref, p_ref, ba_ref, h_ref):
    @pl.when(pl.program_id(1) == 0)
    def _():
        hb = _layer_norm(x_ref[...], g_ref[...], b_ref[...]).astype(BF16)
        h_ref[...] = hb
        ba_ref[...] = jnp.dot(hb, wba_ref[...], preferred_element_type=F32)

    p_ref[...] = jnp.dot(h_ref[...], w_ref[...], preferred_element_type=F32).astype(p_ref.dtype)


def _ln_inproj(x, ln_g, ln_b, w_p, w_ba, *, tm, tn=1024):
    rows = x.shape[0]
    assert rows % tm == 0 and P_WIDTH % tn == 0
    return pl.pallas_call(
        _ln_inproj_kernel,
        grid=(rows // tm, P_WIDTH // tn),
        in_specs=[
            pl.BlockSpec((tm, D_MODEL), lambda i, j: (i, 0)),
            pl.BlockSpec((1, D_MODEL), lambda i, j: (0, 0)),
            pl.BlockSpec((1, D_MODEL), lambda i, j: (0, 0)),
            pl.BlockSpec((D_MODEL, tn), lambda i, j: (0, j)),
            pl.BlockSpec((D_MODEL, LANE), lambda i, j: (0, 0)),
        ],
        out_specs=[
            pl.BlockSpec((tm, tn), lambda i, j: (i, j)),
            pl.BlockSpec((tm, LANE), lambda i, j: (i, 0)),
        ],
        out_shape=[
            jax.ShapeDtypeStruct((rows, P_WIDTH), BF16),
            jax.ShapeDtypeStruct((rows, LANE), F32),
        ],
        scratch_shapes=[pltpu.VMEM((tm, D_MODEL), BF16)],
        compiler_params=pltpu.CompilerParams(
            dimension_semantics=("parallel", "arbitrary"), vmem_limit_bytes=VMEM_LIMIT),
        name="ln_inproj",
    )(x, ln_g, ln_b, w_p, w_ba)


def _split3(x):
    hi = x.astype(BF16)
    r1 = x - hi.astype(F32)
    mid = r1.astype(BF16)
    lo = (r1 - mid.astype(F32)).astype(BF16)
    return hi, mid, lo


def _prep_kernel(p_ref, halo_ref, ba_ref, cw_ref, gpar_ref, qkv_ref, kt_ref, bgc_ref, gt_ref, *, tm, valid_rows):
    x = p_ref[...].astype(F32)
    halo = halo_ref[...]
    cw = cw_ref[...]
    row = lax.broadcasted_iota(jnp.int32, (tm, 1), 0)

    def shifted(s):
        y = pltpu.roll(x, (-s) % tm, axis=0)
        if s < 0:
            for t in range(-s):
                y = jnp.where(row == t, halo[2 + s + t:3 + s + t, :], y)
        else:
            for t in range(s):
                y = jnp.where(row == tm - s + t, halo[2 + t:3 + t, :], y)
        return y

    acc = x * cw[2:3, :]
    for s in (-2, -1, 1, 2):
        acc = acc + shifted(s) * cw[2 + s:3 + s, :]
    act = _silu(acc)

    if valid_rows is not None:
        keep = row < valid_rows
        act = jnp.where(keep, act, 0.0)

    for h in range(HEADS):
        for base, scale in ((0, HEAD_DIM ** -0.5), (DN_WIDTH, 1.0)):
            sl = slice(base + h * HEAD_DIM, base + (h + 1) * HEAD_DIM)
            t = act[:, sl]
            n = lax.rsqrt(jnp.sum(t * t, axis=-1, keepdims=True) + L2_EPS)
            tn = t * (n * scale)
            qkv_ref[:, sl] = tn.astype(qkv_ref.dtype)
            if base:
                kt_ref[h * HEAD_DIM:(h + 1) * HEAD_DIM, :] = tn.T.astype(kt_ref.dtype)
    qkv_ref[:, 2 * DN_WIDTH:] = act[:, 2 * DN_WIDTH:].astype(qkv_ref.dtype)

    ba = ba_ref[...]
    gpar = gpar_ref[...]
    lane = lax.broadcasted_iota(jnp.int32, (tm, LANE), 1)
    beta = jax.nn.sigmoid(ba)
    z = ba + gpar[1:2, :]
    softplus = jnp.maximum(z, 0.0) + jnp.log1p(jnp.exp(-jnp.abs(z)))
    g = -jnp.exp(gpar[0:1, :]) * softplus
    is_g = (lane >= BG_G) & (lane < BG_GC)
    g = jnp.where(is_g, g, 0.0)
    beta = jnp.where(lane < BG_G, beta, 0.0)
    if valid_rows is not None:
        g = jnp.where(keep, g, 0.0)
        beta = jnp.where(keep, beta, 0.0)

    r = lax.broadcasted_iota(jnp.int32, (tm, tm), 0)
    c = lax.broadcasted_iota(jnp.int32, (tm, tm), 1)
    same = (r // CHUNK) == (c // CHUNK)
    m_pre = jnp.where(same & (r >= c), 1.0, 0.0).astype(BF16)
    m_suf = jnp.where(same & (r <= c), 1.0, 0.0).astype(BF16)
    pre = jnp.zeros((tm, LANE), F32)
    suf = jnp.zeros((tm, LANE), F32)
    for part in _split3(g):
        pre = pre + jnp.dot(m_pre, part, preferred_element_type=F32)
        suf = suf + jnp.dot(m_suf, part, preferred_element_type=F32)
    tot = pre + suf - g
    fwd_lane = lane < BG_G + HEADS
    gc = jnp.where(fwd_lane, pre, suf)
    out = beta + g + pltpu.roll(gc, BG_GC - BG_G, axis=1) + pltpu.roll(tot, BG_TOT - BG_G, axis=1)
    bgc_ref[...] = out
    gt_ref[...] = out.T[:GT_ROWS, :]


def _prep(p, halo, ba, conv_w8, gpar, *, batch, nt, tm, valid_rows=None):
    rows = batch * nt * tm
    rmap = lambda b, i: (b * nt + i, 0)
    cmap = lambda b, i: (0, b * nt + i)
    return pl.pallas_call(
        functools.partial(_prep_kernel, tm=tm, valid_rows=valid_rows),
        grid=(batch, nt),
        in_specs=[
            pl.BlockSpec((tm, 3 * DN_WIDTH), rmap),
            pl.BlockSpec((None, None, 8, 3 * DN_WIDTH), lambda b, i: (b, i, 0, 0)),
            pl.BlockSpec((tm, LANE), rmap),
            pl.BlockSpec((8, 3 * DN_WIDTH), lambda b, i: (0, 0)),
            pl.BlockSpec((8, LANE), lambda b, i: (0, 0)),
        ],
        out_specs=[
            pl.BlockSpec((tm, 3 * DN_WIDTH), rmap),
            pl.BlockSpec((DN_WIDTH, tm), cmap),
            pl.BlockSpec((tm, LANE), rmap),
            pl.BlockSpec((GT_ROWS, tm), cmap),
        ],
        out_shape=[
            jax.ShapeDtypeStruct((rows, 3 * DN_WIDTH), BF16),
            jax.ShapeDtypeStruct((DN_WIDTH, rows), BF16),
            jax.ShapeDtypeStruct((rows, LANE), F32),
            jax.ShapeDtypeStruct((GT_ROWS, rows), F32),
        ],
        compiler_params=pltpu.CompilerParams(
            dimension_semantics=("parallel", "parallel"), vmem_limit_bytes=VMEM_LIMIT),
        name="prep",
    )(p, halo, ba, conv_w8, gpar)


def _bd(y):
    yb = y.astype(BF16)
    z = jnp.zeros((HEAD_DIM, HEAD_DIM), BF16)
    top = jnp.concatenate([yb[:, :HEAD_DIM], z], axis=1)
    bot = jnp.concatenate([z, yb[:, HEAD_DIM:]], axis=1)
    return jnp.concatenate([top, bot], axis=0)


def _pair_cols(bg, lane0):
    a = jnp.broadcast_to(bg[:, lane0:lane0 + 1], (CHUNK, HEAD_DIM))
    b = jnp.broadcast_to(bg[:, lane0 + 1:lane0 + 2], (CHUNK, HEAD_DIM))
    return jnp.concatenate([a, b], axis=1)


def _pair_rows(gt, row0, rs):
    return jnp.concatenate([gt[row0:row0 + 1, rs], gt[row0 + 1:row0 + 2, rs]], axis=1)


def _delta_kernel(qf_ref, kf_ref, vf_ref, ktf_ref, bgf_ref, gtf_ref,
                  qb_ref, kb_ref, vb_ref, ktb_ref, bgb_ref, gtb_ref, s0_ref,
                  of_ref, ob_ref, sfin_ref,
                  s_scr, pk_scr, kq_scr, bv_scr, gl_scr, *, nchunk):
    i = pl.program_id(1)

    @pl.when(i == 0)
    def _():
        s_scr[0] = s0_ref[...]
        s_scr[1] = jnp.zeros_like(s_scr[1])

    row = lax.broadcasted_iota(jnp.int32, (CHUNK, PAIR), 0)
    col = lax.broadcasted_iota(jnp.int32, (CHUNK, PAIR), 1) % HEAD_DIM
    eye2 = jnp.where(row == col, 1.0, 0.0).astype(F32)
    dirs = (
        dict(q=qf_ref, k=kf_ref, v=vf_ref, kt=ktf_ref, bg=bgf_ref, gt=gtf_ref, o=of_ref, dcol=0,
             incl=row >= col, strict=row > col, order=tuple(range(nchunk))),
        dict(q=qb_ref, k=kb_ref, v=vb_ref, kt=ktb_ref, bg=bgb_ref, gt=gtb_ref, o=ob_ref, dcol=HEADS,
             incl=row <= col, strict=row < col, order=tuple(range(nchunk - 1, -1, -1))),
    )
    units = [(d, step, p) for d in range(2) for step in range(nchunk) for p in range(NPAIR)]
    steps = int(math.log2(INV_BASE)) - 1
    same_block = {}
    size = INV_BASE
    while size < CHUNK:
        same_block[size] = (row // size) == (col // size)
        size *= 2

    def phase_a(group):
        st = []
        for u, (d, step, p) in group:
            dr = dirs[d]
            c = dr["order"][step]
            rs = slice(c * CHUNK, (c + 1) * CHUNK)
            ls = slice(p * PAIR, (p + 1) * PAIR)
            j0 = dr["dcol"] + 2 * p
            bg = dr["bg"][rs, :]
            gt = dr["gt"]
            beta = _pair_cols(bg, BG_BETA + j0)
            gc = _pair_cols(bg, BG_GC + j0)
            tot = _pair_cols(bg, BG_TOT + j0)
            gr = _pair_rows(gt, BG_GC + j0, rs)
            totr = _pair_rows(gt, BG_TOT + j0, rs)
            decay = jnp.exp(jnp.where(dr["incl"], gc - gr, -1e30))
            kt2 = jnp.concatenate([dr["kt"][p * PAIR:p * PAIR + HEAD_DIM, rs],
                                   dr["kt"][p * PAIR + HEAD_DIM:(p + 1) * PAIR, rs]], axis=1)
            qb = dr["q"][rs, ls]
            kb = dr["k"][rs, ls]
            qk = jnp.dot(jnp.concatenate([qb, kb], axis=0), _bd(kt2), preferred_element_type=F32)
            lm = jnp.where(dr["strict"], beta * qk[CHUNK:] * decay, 0.0)
            egc = jnp.exp(gc)
            kq_scr[u] = jnp.concatenate([(beta * egc) * kb.astype(F32), egc * qb.astype(F32)],
                                        axis=0).astype(BF16)
            bv_scr[u] = beta * dr["v"][rs, ls].astype(F32)
            gl_scr[u] = jnp.exp(tot[0:8, :])
            st.append(dict(lm=lm, pm=qk[:CHUNK] * decay, kdt=kt2.astype(F32) * jnp.exp(totr - gr)))
        for s in st:
            s["x"] = -jnp.where(same_block[INV_BASE], s["lm"], 0.0)
            s["t"] = eye2 + s["x"]
        for s in st:
            s["p"] = _mm(s["x"], _bd(s["x"]))
        for it in range(steps):
            for s in st:
                s["t"] = s["t"] + _mm(s["t"], _bd(s["p"]))
            if it + 1 < steps:
                for s in st:
                    s["p"] = _mm(s["p"], _bd(s["p"]))
        size = INV_BASE
        while size < CHUNK:
            join = ~same_block[size] if 2 * size == CHUNK else same_block[2 * size] & ~same_block[size]
            for s in st:
                s["a"] = _mm(jnp.where(join, s["lm"], 0.0), _bd(s["t"]))
            for s in st:
                s["t"] = s["t"] - _mm(s["t"], _bd(s["a"]))
            size *= 2
        for (u, _), s in zip(group, st):
            pk_scr[u] = _mm(jnp.concatenate([s["pm"], s["kdt"]], axis=0), _bd(s["t"])).astype(BF16)

    indexed = list(enumerate(units))
    for d in range(2):
        phase_a([(u, un) for u, un in indexed if un[0] == d])

    for step in range(nchunk):
        active = [(u, un) for u, un in indexed if un[1] == step]
        xs = [jnp.dot(kq_scr[u], _bd(s_scr[d, p]), preferred_element_type=F32) for u, (d, _, p) in active]
        rr = [bv_scr[u] - x[:CHUNK] for (u, _), x in zip(active, xs)]
        for (u, (d, _, p)), x, r in zip(active, xs, rr):
            dr = dirs[d]
            c = dr["order"][step]
            res = jnp.dot(pk_scr[u], _bd(r), preferred_element_type=F32)
            dr["o"][c * CHUNK:(c + 1) * CHUNK, p * PAIR:(p + 1) * PAIR] = (
                x[CHUNK:] + res[:CHUNK]).astype(dr["o"].dtype)
            s_scr[d, p] = gl_scr[u][0:1, :] * s_scr[d, p] + res[CHUNK:]

    @pl.when(i == pl.num_programs(1) - 1)
    def _():
        sfin_ref[...] = s_scr[0]


def _delta(qkv, kt, bgc, gt, s0, *, batch, nblk, tb):
    rows = batch * nblk * tb
    assert tb % CHUNK == 0 and CHUNK == HEAD_DIM
    nchunk = tb // CHUNK
    nunit = 2 * nchunk * NPAIR
    fwd = lambda b, i: b * nblk + i
    bwd = lambda b, i: b * nblk + (nblk - 1 - i)

    def dir_specs(blk):
        return [
            pl.BlockSpec((tb, DN_WIDTH), lambda b, i: (blk(b, i), 0)),
            pl.BlockSpec((tb, DN_WIDTH), lambda b, i: (blk(b, i), 1)),
            pl.BlockSpec((tb, DN_WIDTH), lambda b, i: (blk(b, i), 2)),
            pl.BlockSpec((DN_WIDTH, tb), lambda b, i: (0, blk(b, i))),
            pl.BlockSpec((tb, LANE), lambda b, i: (blk(b, i), 0)),
            pl.BlockSpec((GT_ROWS, tb), lambda b, i: (0, blk(b, i))),
        ]

    sspec = pl.BlockSpec((None, NPAIR, HEAD_DIM, PAIR), lambda b, i: (b, 0, 0, 0))
    return pl.pallas_call(
        functools.partial(_delta_kernel, nchunk=nchunk),
        grid=(batch, nblk),
        in_specs=dir_specs(fwd) + dir_specs(bwd) + [sspec],
        out_specs=[
            pl.BlockSpec((tb, DN_WIDTH), lambda b, i: (fwd(b, i), 0)),
            pl.BlockSpec((tb, DN_WIDTH), lambda b, i: (bwd(b, i), 0)),
            sspec,
        ],
        out_shape=[
            jax.ShapeDtypeStruct((rows, DN_WIDTH), BF16),
            jax.ShapeDtypeStruct((rows, DN_WIDTH), BF16),
            jax.ShapeDtypeStruct((batch, NPAIR, HEAD_DIM, PAIR), F32),
        ],
        scratch_shapes=[
            pltpu.VMEM((2, NPAIR, HEAD_DIM, PAIR), F32),
            pltpu.VMEM((nunit, CHUNK + HEAD_DIM, PAIR), BF16),
            pltpu.VMEM((nunit, 2 * CHUNK, PAIR), BF16),
            pltpu.VMEM((nunit, CHUNK, PAIR), F32),
            pltpu.VMEM((nunit, 8, PAIR), F32),
        ],
        compiler_params=pltpu.CompilerParams(
            dimension_semantics=("parallel", "arbitrary"), vmem_limit_bytes=VMEM_LIMIT),
        name="delta",
    )(qkv, qkv, qkv, kt, bgc, gt, qkv, qkv, qkv, kt, bgc, gt, s0)


def _dft_tables(n):
    k = np.arange(n, dtype=np.int64)
    ang = 2.0 * np.pi * ((k[:, None] * k[None, :]) % n).astype(np.float64) / n
    return np.cos(ang), np.sin(ang)


def _fft_factor(t):
    if t <= 4096:
        return t, 1
    best = None
    for n1 in range(16, t, 16):
        if t % n1:
            continue
        n2 = t // n1
        n2p = -(-n2 // 16) * 16
        cost = 4 * n1 + 2 * n2p
        if best is None or cost < best[0]:
            best = (cost, n1, n2)
    assert best is not None, t
    return best[1], best[2]


def _caxis_kernel(f_ref, cs_ref, a_ref, b_ref):
    cs = cs_ref[...]
    for g in range(FN_GROUPS):
        sl = slice(g * FN_GROUP_DIM, (g + 1) * FN_GROUP_DIM)
        ab = jnp.dot(f_ref[:, sl], cs, preferred_element_type=F32)
        a_ref[:, sl] = ab[:, :FN_GROUP_DIM].astype(a_ref.dtype)
        b_ref[:, sl] = ab[:, FN_GROUP_DIM:].astype(b_ref.dtype)


def _caxis(f, cs):
    rows = f.shape[0]
    tr = _row_tile(rows, 1024)
    spec = pl.BlockSpec((tr, FN_WIDTH), lambda i: (i, 0))
    return pl.pallas_call(
        _caxis_kernel,
        grid=(rows // tr,),
        in_specs=[spec, pl.BlockSpec((FN_GROUP_DIM, 2 * FN_GROUP_DIM), lambda i: (0, 0))],
        out_specs=[spec, spec],
        out_shape=[jax.ShapeDtypeStruct((rows, FN_WIDTH), BF16)] * 2,
        compiler_params=pltpu.CompilerParams(
            dimension_semantics=("parallel",), vmem_limit_bytes=VMEM_LIMIT),
        name="dft_channel",
    )(f, cs)


def _dft1_dense_kernel(a_ref, b_ref, c_ref, s_ref, o_ref, *, scale):
    y = (jnp.dot(c_ref[...], a_ref[...], preferred_element_type=F32)
         - jnp.dot(s_ref[...], b_ref[...], preferred_element_type=F32))
    o_ref[...] = (y * scale).astype(o_ref.dtype)


def _dft1_kernel(a_ref, b_ref, c_ref, s_ref, twr_ref, twi_ref, yre_ref, yim_ref, *, n2):
    t2 = pl.program_id(1)

    @pl.when(t2 < n2)
    def _():
        a = a_ref[...]
        b = b_ref[...]
        c = c_ref[...]
        s = s_ref[...]
        yre = jnp.dot(c, a, preferred_element_type=F32) - jnp.dot(s, b, preferred_element_type=F32)
        yim = -(jnp.dot(c, b, preferred_element_type=F32) + jnp.dot(s, a, preferred_element_type=F32))
        wr = twr_ref[...]
        wi = twi_ref[...]
        for l in range(FN_WIDTH // LANE):
            sl = slice(l * LANE, (l + 1) * LANE)
            yre_ref[:, sl] = (yre[:, sl] * wr - yim[:, sl] * wi).astype(yre_ref.dtype)
            yim_ref[:, sl] = (yre[:, sl] * wi + yim[:, sl] * wr).astype(yim_ref.dtype)

    @pl.when(t2 >= n2)
    def _():
        yre_ref[...] = jnp.zeros_like(yre_ref)
        yim_ref[...] = jnp.zeros_like(yim_ref)


def _dft2_kernel(yre_ref, yim_ref, c_ref, s_ref, o_ref, *, n2, scale):
    y = (jnp.dot(c_ref[...], yre_ref[...], preferred_element_type=F32)
         + jnp.dot(s_ref[...], yim_ref[...], preferred_element_type=F32))
    o_ref[...] = (y[:n2] * scale).astype(o_ref.dtype)


def _fourier_time(a, b, *, batch, t, factor=None):
    n1, n2 = factor if factor is not None else _fft_factor(t)
    assert n1 * n2 == t
    scale = 1.0 / math.sqrt(t * FN_GROUP_DIM)
    c1, s1 = _dft_tables(n1)
    c1 = jnp.asarray(c1, BF16)
    s1 = jnp.asarray(s1, BF16)
    if n2 == 1:
        tmi = _row_tile(n1, 1024)
        return pl.pallas_call(
            functools.partial(_dft1_dense_kernel, scale=scale),
            grid=(batch, n1 // tmi),
            in_specs=[
                pl.BlockSpec((None, n1, FN_WIDTH), lambda bb, m: (bb, 0, 0)),
                pl.BlockSpec((None, n1, FN_WIDTH), lambda bb, m: (bb, 0, 0)),
                pl.BlockSpec((tmi, n1), lambda bb, m: (m, 0)),
                pl.BlockSpec((tmi, n1), lambda bb, m: (m, 0)),
            ],
            out_specs=pl.BlockSpec((None, tmi, FN_WIDTH), lambda bb, m: (bb, m, 0)),
            out_shape=jax.ShapeDtypeStruct((batch, t, FN_WIDTH), BF16),
            compiler_params=pltpu.CompilerParams(
                dimension_semantics=("parallel", "parallel"), vmem_limit_bytes=VMEM_LIMIT),
            name="dft_time_dense",
        )(a, b, c1, s1)

    n2p = -(-n2 // 16) * 16
    k1 = np.arange(n1, dtype=np.int64)
    t2 = np.arange(n2p, dtype=np.int64)
    ang = 2.0 * np.pi * ((t2[:, None] * k1[None, :]) % t).astype(np.float64) / t
    twr = jnp.asarray(np.broadcast_to(np.cos(ang)[:, :, None], (n2p, n1, LANE)), F32)
    twi = jnp.asarray(np.broadcast_to(-np.sin(ang)[:, :, None], (n2p, n1, LANE)), F32)
    a2 = a.reshape(batch, n1, n2 * FN_WIDTH)
    b2 = b.reshape(batch, n1, n2 * FN_WIDTH)
    in_col = lambda bb, j: (bb, 0, jnp.minimum(j, n2 - 1))
    yre, yim = pl.pallas_call(
        functools.partial(_dft1_kernel, n2=n2),
        grid=(batch, n2p),
        in_specs=[
            pl.BlockSpec((None, n1, FN_WIDTH), in_col),
            pl.BlockSpec((None, n1, FN_WIDTH), in_col),
            pl.BlockSpec((n1, n1), lambda bb, j: (0, 0)),
            pl.BlockSpec((n1, n1), lambda bb, j: (0, 0)),
            pl.BlockSpec((None, n1, LANE), lambda bb, j: (j, 0, 0)),
            pl.BlockSpec((None, n1, LANE), lambda bb, j: (j, 0, 0)),
        ],
        out_specs=[pl.BlockSpec((None, None, n1, FN_WIDTH), lambda bb, j: (bb, j, 0, 0))] * 2,
        out_shape=[jax.ShapeDtypeStruct((batch, n2p, n1, FN_WIDTH), BF16)] * 2,
        compiler_params=pltpu.CompilerParams(
            dimension_semantics=("parallel", "parallel"), vmem_limit_bytes=VMEM_LIMIT),
        name="dft_time_1",
    )(a2, b2, c1, s1, twr, twi)

    c2 = np.zeros((n2p, n2p))
    s2 = np.zeros((n2p, n2p))
    c2[:n2, :n2], s2[:n2, :n2] = _dft_tables(n2)
    width = n1 * FN_WIDTH
    tn = 2048
    assert width % tn == 0
    out = pl.pallas_call(
        functools.partial(_dft2_kernel, n2=n2, scale=scale),
        grid=(batch, width // tn),
        in_specs=[
            pl.BlockSpec((None, n2p, tn), lambda bb, j: (bb, 0, j)),
            pl.BlockSpec((None, n2p, tn), lambda bb, j: (bb, 0, j)),
            pl.BlockSpec((n2p, n2p), lambda bb, j: (0, 0)),
            pl.BlockSpec((n2p, n2p), lambda bb, j: (0, 0)),
        ],
        out_specs=pl.BlockSpec((None, n2, tn), lambda bb, j: (bb, 0, j)),
        out_shape=jax.ShapeDtypeStruct((batch, n2, width), BF16),
        compiler_params=pltpu.CompilerParams(
            dimension_semantics=("parallel", "parallel"), vmem_limit_bytes=VMEM_LIMIT),
        name="dft_time_2",
    )(yre.reshape(batch, n2p, width), yim.reshape(batch, n2p, width),
      jnp.asarray(c2, BF16), jnp.asarray(s2, BF16))
    return out.reshape(batch, t, FN_WIDTH)


def _merge_kernel(x_ref, of_ref, ob_ref, za_ref, zf_ref, fr_ref, lig_ref, lib_ref, nw_ref,
                  wga_ref, wgf_ref, bga_ref, bgf_ref, wa_ref, wf_ref, wo_ref, lg_ref, lb_ref,
                  y_ref, hb_ref, ya_ref, yf_ref, acc_ref):
    j = pl.program_id(1)

    @pl.when(j == 0)
    def _():
        hb_ref[...] = _layer_norm(x_ref[...], lig_ref[...], lib_ref[...]).astype(BF16)
        nw = nw_ref[...]
        for h in range(HEADS):
            hs = slice(h * HEAD_DIM, (h + 1) * HEAD_DIM)
            o = of_ref[:, hs].astype(F32) + ob_ref[:, hs].astype(F32)
            o = o * lax.rsqrt(jnp.mean(o * o, axis=-1, keepdims=True) + RMS_EPS) * nw
            ya_ref[:, hs] = (o * _silu(za_ref[:, hs].astype(F32))).astype(BF16)
        yf_ref[...] = (fr_ref[...].astype(F32) * _silu(zf_ref[...].astype(F32))).astype(BF16)
        acc_ref[...] = jnp.zeros_like(acc_ref)

    hb = hb_ref[...]
    ga = jax.nn.sigmoid(jnp.dot(hb, wga_ref[...], preferred_element_type=F32) + bga_ref[...])
    gf = jax.nn.sigmoid(jnp.dot(hb, wgf_ref[...], preferred_element_type=F32) + bgf_ref[...])
    m = (ga * jnp.dot(ya_ref[...], wa_ref[...], preferred_element_type=F32)
         + gf * jnp.dot(yf_ref[...], wf_ref[...], preferred_element_type=F32))
    acc_ref[...] += jnp.dot(m.astype(BF16), wo_ref[...], preferred_element_type=F32)

    @pl.when(j == pl.num_programs(1) - 1)
    def _():
        h = _layer_norm(x_ref[...], lig_ref[...], lib_ref[...])
        y_ref[...] = _layer_norm(DEEPNORM_ALPHA * h + acc_ref[...], lg_ref[...], lb_ref[...])


def _merge(x, o_f, o_b, p, fr, ln_in_g, ln_in_b, nw, w_g, b_g, w_a, w_f, w_o, ln_g, ln_b, *, tm=512, tn=256):
    rows = x.shape[0]
    assert rows % tm == 0 and D_MODEL % tn == 0
    nj = D_MODEL // tn
    row_spec = lambda w, col=0: pl.BlockSpec((tm, w), lambda i, j: (i, col))
    vec_spec = lambda w: pl.BlockSpec((1, w), lambda i, j: (0, 0))
    return pl.pallas_call(
        _merge_kernel,
        grid=(rows // tm, nj),
        in_specs=[
            row_spec(D_MODEL),
            row_spec(DN_WIDTH), row_spec(DN_WIDTH),
            row_spec(DN_WIDTH, PCOL_ZA), row_spec(FN_WIDTH, PCOL_ZF),
            row_spec(FN_WIDTH),
            vec_spec(D_MODEL), vec_spec(D_MODEL), vec_spec(HEAD_DIM),
            pl.BlockSpec((D_MODEL, tn), lambda i, j: (0, j)),
            pl.BlockSpec((D_MODEL, tn), lambda i, j: (0, nj + j)),
            pl.BlockSpec((1, tn), lambda i, j: (0, j)),
            pl.BlockSpec((1, tn), lambda i, j: (0, nj + j)),
            pl.BlockSpec((DN_WIDTH, tn), lambda i, j: (0, j)),
            pl.BlockSpec((FN_WIDTH, tn), lambda i, j: (0, j)),
            pl.BlockSpec((tn, D_MODEL), lambda i, j: (j, 0)),
            vec_spec(D_MODEL), vec_spec(D_MODEL),
        ],
        out_specs=pl.BlockSpec((tm, D_MODEL), lambda i, j: (i, 0)),
        out_shape=jax.ShapeDtypeStruct((rows, D_MODEL), F32),
        scratch_shapes=[
            pltpu.VMEM((tm, D_MODEL), BF16),
            pltpu.VMEM((tm, DN_WIDTH), BF16),
            pltpu.VMEM((tm, FN_WIDTH), BF16),
            pltpu.VMEM((tm, D_MODEL), F32),
        ],
        compiler_params=pltpu.CompilerParams(
            dimension_semantics=("parallel", "arbitrary"), vmem_limit_bytes=VMEM_LIMIT),
        name="merge",
    )(x, o_f, o_b, p, p, fr, ln_in_g, ln_in_b, nw, w_g, w_g, b_g, b_g, w_a, w_f, w_o, ln_g, ln_b)


def _encode(x, p_meta, ba_meta, wts, *, fft_factor=None, tm_rows=1024):
    batch, seq, _ = x.shape
    rows = batch * seq
    t_full = N_META + seq
    x2 = x.reshape(rows, D_MODEL)

    p, ba = _ln_inproj(x2, wts["ln_in_g"], wts["ln_in_b"], wts["w_p"], wts["w_ba"],
                       tm=_row_tile(rows, tm_rows))

    tb = DELTA_BLOCK
    assert seq % tb == 0 and seq >= META_BLOCK
    nt = seq // tb
    p4 = p.reshape(batch, nt, tb, P_WIDTH)
    tail = p4[:, :, tb - 2:, :3 * DN_WIDTH].astype(F32)
    head = p4[:, :, :2, :3 * DN_WIDTH].astype(F32)
    meta_tail = jnp.broadcast_to(p_meta[None, None, N_META - 2:, :3 * DN_WIDTH].astype(F32),
                                 (batch, 1, 2, 3 * DN_WIDTH))
    prev = jnp.concatenate([meta_tail, tail[:, :-1]], axis=1)
    nxt = jnp.concatenate([head[:, 1:], jnp.zeros((batch, 1, 2, 3 * DN_WIDTH), F32)], axis=1)
    halo = jnp.concatenate([prev, nxt, jnp.zeros((batch, nt, 4, 3 * DN_WIDTH), F32)], axis=2)
    qkv, kt, bgc, gt = _prep(p, halo, ba, wts["conv_w8"], wts["gpar"], batch=batch, nt=nt, tm=tb)

    mb = META_BLOCK
    p_m = jnp.concatenate([
        jnp.broadcast_to(p_meta[None, :, :3 * DN_WIDTH], (batch, N_META, 3 * DN_WIDTH)),
        p.reshape(batch, seq, P_WIDTH)[:, :mb - N_META, :3 * DN_WIDTH]], axis=1).reshape(batch * mb, 3 * DN_WIDTH)
    ba_m = jnp.concatenate([
        jnp.broadcast_to(ba_meta[None], (batch, N_META, LANE)),
        ba.reshape(batch, seq, LANE)[:, :mb - N_META]], axis=1).reshape(batch * mb, LANE)
    halo_m = jnp.zeros((batch, 1, 8, 3 * DN_WIDTH), F32)
    qkv_m, kt_m, bgc_m, gt_m = _prep(p_m, halo_m, ba_m, wts["conv_w8"], wts["gpar"],
                                     batch=batch, nt=1, tm=mb, valid_rows=N_META)
    s_zero = jnp.zeros((batch, NPAIR, HEAD_DIM, PAIR), F32)
    _, _, s_meta = _delta(qkv_m, kt_m, bgc_m, gt_m, s_zero, batch=batch, nblk=1, tb=mb)

    o_f, o_b, _ = _delta(qkv, kt, bgc, gt, s_meta, batch=batch, nblk=nt, tb=tb)

    f_x = p.reshape(batch, seq, P_WIDTH)[:, :, PCOL_F * 1024:(PCOL_F + 1) * 1024]
    f_m = jnp.broadcast_to(p_meta[None, :, PCOL_F * 1024:(PCOL_F + 1) * 1024], (batch, N_META, FN_WIDTH))
    f_full = jnp.concatenate([f_m, f_x], axis=1).reshape(batch * t_full, FN_WIDTH)
    a, b = _caxis(f_full, wts["cs"])
    fr = _fourier_time(a.reshape(batch, t_full, FN_WIDTH), b.reshape(batch, t_full, FN_WIDTH),
                       batch=batch, t=t_full, factor=fft_factor)
    fr = fr[:, N_META:].reshape(rows, FN_WIDTH)

    y = _merge(x2, o_f, o_b, p, fr, wts["ln_in_g"], wts["ln_in_b"], wts["nw"], wts["w_g"], wts["b_g"],
               wts["w_a"], wts["w_f"], wts["w_o"], wts["ln_g"], wts["ln_b"], tm=_row_tile(rows, 512))
    return y.reshape(batch, seq, D_MODEL)


def _prepare_weights(ln_in_g, ln_in_b, w_in, b_gate, conv_w, a_log, dt_bias, dn_norm_w,
                     w_proj_a, w_proj_f, w_out, ln_g, ln_b):
    w = w_in[0]
    col_b = 4 * DN_WIDTH
    col_f = col_b + 4 * HEADS
    col_g = col_f + 2 * FN_WIDTH
    w_p = jnp.concatenate([w[:, :col_b], w[:, col_f:col_g]], axis=1).astype(BF16)
    w_ba = jnp.pad(w[:, col_b:col_f], ((0, 0), (0, LANE - 4 * HEADS))).astype(BF16)
    gpar = jnp.zeros((8, LANE), F32)
    gpar = gpar.at[0, BG_G:BG_GC].set(a_log[0].reshape(-1).astype(F32))
    gpar = gpar.at[1, BG_G:BG_GC].set(dt_bias[0].reshape(-1).astype(F32))
    k = np.arange(FN_GROUP_DIM, dtype=np.int64)
    ang = 2.0 * np.pi * ((k[:, None] * k[None, :]) % FN_GROUP_DIM) / FN_GROUP_DIM
    cs = jnp.asarray(np.concatenate([np.cos(ang), np.sin(ang)], axis=1), BF16)
    return {
        "ln_in_g": ln_in_g.reshape(1, D_MODEL).astype(F32),
        "ln_in_b": ln_in_b.reshape(1, D_MODEL).astype(F32),
        "w_p": w_p,
        "w_ba": w_ba,
        "conv_w8": jnp.pad(conv_w[0].astype(F32), ((0, 8 - CONV_K), (0, 0))),
        "gpar": gpar,
        "cs": cs,
        "nw": dn_norm_w[0].reshape(1, HEAD_DIM).astype(F32),
        "w_g": w[:, col_g:].astype(BF16),
        "b_g": b_gate[0].reshape(1, 2 * D_MODEL).astype(F32),
        "w_a": w_proj_a[0].astype(BF16),
        "w_f": w_proj_f[0].astype(BF16),
        "w_o": w_out[0].astype(BF16),
        "ln_g": ln_g[0].reshape(1, D_MODEL).astype(F32),
        "ln_b": ln_b[0].reshape(1, D_MODEL).astype(F32),
    }


def kernel(x_prompt, x_sample, meta_tokens, ln_in_g, ln_in_b, w_in, b_gate, conv_w, a_log, dt_bias, dn_norm_w,
           w_proj_a, w_proj_f, w_out, ln_g, ln_b):
    wts = _prepare_weights(ln_in_g, ln_in_b, w_in, b_gate, conv_w, a_log, dt_bias, dn_norm_w,
                           w_proj_a, w_proj_f, w_out, ln_g, ln_b)
    p_meta, ba_meta = _ln_inproj(meta_tokens.astype(F32), wts["ln_in_g"], wts["ln_in_b"],
                                 wts["w_p"], wts["w_ba"], tm=N_META)
    y_prompt = _encode(x_prompt, p_meta, ba_meta, wts)
    y_sample = _encode(x_sample, p_meta, ba_meta, wts)
    return (y_prompt, y_sample)
```

```python
import functools
import math

import numpy as np
import jax
import jax.numpy as jnp
from jax import lax
from jax.experimental import pallas as pl
from jax.experimental.pallas import tpu as pltpu

F32 = jnp.float32
BF16 = jnp.bfloat16

D_MODEL = 2048
N_META = 16
HEADS = 8
HEAD_DIM = 128
DN_WIDTH = HEADS * HEAD_DIM
CONV_K = 5
FN_GROUPS = 4
FN_GROUP_DIM = 256
FN_WIDTH = FN_GROUPS * FN_GROUP_DIM
DEPTH = 1
DEEPNORM_ALPHA = (2 * DEPTH) ** 0.25
LN_EPS = 1e-5
RMS_EPS = 1e-6
L2_EPS = 1e-6

LANE = 128
CHUNK = HEAD_DIM
PAIR = 2 * HEAD_DIM
NPAIR = HEADS // 2
INV_BASE = 16
DELTA_BLOCK = 256
META_BLOCK = CHUNK
T2_GROUP = 16
VMEM_LIMIT = 56 * 1024 * 1024

PCOL_Q, PCOL_K, PCOL_V, PCOL_ZA, PCOL_F, PCOL_ZF = range(6)
P_WIDTH = 6 * 1024
BG_BETA, BG_G, BG_GC, BG_TOT = 0, 16, 32, 48
BA_MU, BA_RSTD = 32, 33
GT_ROWS = 64


def _mm(a, b):
    return jnp.dot(a.astype(BF16), b.astype(BF16), preferred_element_type=F32)


def _layer_norm(x, g, b):
    mu = jnp.mean(x, axis=-1, keepdims=True)
    xc = x - mu
    var = jnp.mean(xc * xc, axis=-1, keepdims=True)
    return xc * lax.rsqrt(var + LN_EPS) * g + b


def _silu(x):
    return x * jax.nn.sigmoid(x)


def _row_tile(rows, cap, mult=16):
    best = None
    for t in range(mult, min(rows, cap) + 1, mult):
        if rows % t == 0:
            best = t
    assert best is not None, (rows, cap, mult)
    return best


def _ln_inproj_kernel(x_ref, g_ref, b_ref, w_ref, wba_ref, p_ref, ba_ref, hb_ref):
    @pl.when(pl.program_id(1) == 0)
    def _():
        x = x_ref[...]
        mu = jnp.mean(x, axis=-1, keepdims=True)
        xc = x - mu
        rstd = lax.rsqrt(jnp.mean(xc * xc, axis=-1, keepdims=True) + LN_EPS)
        hb = (xc * rstd * g_ref[...] + b_ref[...]).astype(BF16)
        hb_ref[...] = hb
        lane = lax.broadcasted_iota(jnp.int32, ba_ref.shape, 1)
        ba = jnp.dot(hb, wba_ref[...], preferred_element_type=F32)
        ba_ref[...] = jnp.where(lane == BA_MU, mu, jnp.where(lane == BA_RSTD, rstd, ba))

    p_ref[...] = jnp.dot(hb_ref[...], w_ref[...], preferred_element_type=F32).astype(p_ref.dtype)


def _ln_inproj(x, ln_g, ln_b, w_p, w_ba, *, tm, tn=1024):
    rows = x.shape[0]
    assert rows % tm == 0 and P_WIDTH % tn == 0
    return pl.pallas_call(
        _ln_inproj_kernel,
        grid=(rows // tm, P_WIDTH // tn),
        in_specs=[
            pl.BlockSpec((tm, D_MODEL), lambda i, j: (i, 0)),
            pl.BlockSpec((1, D_MODEL), lambda i, j: (0, 0)),
            pl.BlockSpec((1, D_MODEL), lambda i, j: (0, 0)),
            pl.BlockSpec((D_MODEL, tn), lambda i, j: (0, j)),
            pl.BlockSpec((D_MODEL, LANE), lambda i, j: (0, 0)),
        ],
        out_specs=[
            pl.BlockSpec((tm, tn), lambda i, j: (i, j)),
            pl.BlockSpec((tm, LANE), lambda i, j: (i, 0)),
            pl.BlockSpec((tm, D_MODEL), lambda i, j: (i, 0)),
        ],
        out_shape=[
            jax.ShapeDtypeStruct((rows, P_WIDTH), BF16),
            jax.ShapeDtypeStruct((rows, LANE), F32),
            jax.ShapeDtypeStruct((rows, D_MODEL), BF16),
        ],
        compiler_params=pltpu.CompilerParams(
            dimension_semantics=("parallel", "arbitrary"), vmem_limit_bytes=VMEM_LIMIT),
        name="ln_inproj",
    )(x, ln_g, ln_b, w_p, w_ba)


def _split3(x):
    hi = x.astype(BF16)
    r1 = x - hi.astype(F32)
    mid = r1.astype(BF16)
    lo = (r1 - mid.astype(F32)).astype(BF16)
    return hi, mid, lo


def _prep_kernel(p_ref, halo_ref, ba_ref, cw_ref, gpar_ref, qkv_ref, kt_ref, bgc_ref, gt_ref, *, tm, valid_rows):
    cw = cw_ref[...]
    row = lax.broadcasted_iota(jnp.int32, (tm, 1), 0)
    if valid_rows is not None:
        keep = row < valid_rows

    shifts = (-2, -1, 1, 2)
    r = lax.broadcasted_iota(jnp.int32, (tm, tm), 0)
    c = lax.broadcasted_iota(jnp.int32, (tm, tm), 1)
    rh = lax.broadcasted_iota(jnp.int32, (tm, 16), 0)
    ch = lax.broadcasted_iota(jnp.int32, (tm, 16), 1)
    sel = jnp.concatenate([jnp.where(c == r + s, 1.0, 0.0) for s in shifts], axis=0).astype(BF16)
    sel_halo = jnp.concatenate(
        [jnp.where(((rh + s < 0) & (ch == 8 + rh + s)) | ((rh + s >= tm) & (ch == 8 + rh + s - tm)), 1.0, 0.0)
         for s in shifts], axis=0).astype(BF16)
    def shifted(j):
        ls = slice(j * PAIR, (j + 1) * PAIR)
        return (jnp.dot(sel, p_ref[:, ls], preferred_element_type=F32)
                + jnp.dot(sel_halo, halo_ref[:, ls], preferred_element_type=F32))

    nchunks = 3 * DN_WIDTH // PAIR
    ahead = shifted(0)
    for j in range(nchunks):
        ls = slice(j * PAIR, (j + 1) * PAIR)
        moved = ahead
        if j + 1 < nchunks:
            ahead = shifted(j + 1)
        acc = p_ref[:, ls].astype(F32) * cw[2:3, ls]
        for n, s in enumerate(shifts):
            acc = acc + moved[n * tm:(n + 1) * tm] * cw[2 + s:3 + s, ls]
        act = _silu(acc)
        if valid_rows is not None:
            act = jnp.where(keep, act, 0.0)
        if j * PAIR >= 2 * DN_WIDTH:
            qkv_ref[:, ls] = act.astype(qkv_ref.dtype)
            continue
        scale = HEAD_DIM ** -0.5 if j * PAIR < DN_WIDTH else 1.0
        for half in range(PAIR // HEAD_DIM):
            t = act[:, half * HEAD_DIM:(half + 1) * HEAD_DIM]
            tn = t * (lax.rsqrt(jnp.sum(t * t, axis=-1, keepdims=True) + L2_EPS) * scale)
            col = j * PAIR + half * HEAD_DIM
            qkv_ref[:, col:col + HEAD_DIM] = tn.astype(qkv_ref.dtype)
            if col >= DN_WIDTH:
                kt_ref[col - DN_WIDTH:col - DN_WIDTH + HEAD_DIM, :] = tn.T.astype(kt_ref.dtype)

    ba = ba_ref[...]
    gpar = gpar_ref[...]
    lane = lax.broadcasted_iota(jnp.int32, (tm, LANE), 1)
    beta = jax.nn.sigmoid(ba)
    z = ba + gpar[1:2, :]
    softplus = jnp.maximum(z, 0.0) + jnp.log1p(jnp.exp(-jnp.abs(z)))
    g = -jnp.exp(gpar[0:1, :]) * softplus
    is_g = (lane >= BG_G) & (lane < BG_GC)
    g = jnp.where(is_g, g, 0.0)
    beta = jnp.where(lane < BG_G, beta, 0.0)
    if valid_rows is not None:
        g = jnp.where(keep, g, 0.0)
        beta = jnp.where(keep, beta, 0.0)

    r = lax.broadcasted_iota(jnp.int32, (tm, tm), 0)
    c = lax.broadcasted_iota(jnp.int32, (tm, tm), 1)
    same = (r // CHUNK) == (c // CHUNK)
    m_pre = jnp.where(same & (r >= c), 1.0, 0.0).astype(BF16)
    m_suf = jnp.where(same & (r <= c), 1.0, 0.0).astype(BF16)
    pre = jnp.zeros((tm, LANE), F32)
    suf = jnp.zeros((tm, LANE), F32)
    for part in _split3(g):
        pre = pre + jnp.dot(m_pre, part, preferred_element_type=F32)
        suf = suf + jnp.dot(m_suf, part, preferred_element_type=F32)
    tot = pre + suf - g
    fwd_lane = lane < BG_G + HEADS
    gc = jnp.where(fwd_lane, pre, suf)
    out = beta + g + pltpu.roll(gc, BG_GC - BG_G, axis=1) + pltpu.roll(tot, BG_TOT - BG_G, axis=1)
    bgc_ref[...] = out
    gt_ref[...] = out.T[:GT_ROWS, :]


def _prep(p, halo, ba, conv_w8, gpar, *, batch, nt, tm, valid_rows=None):
    rows = batch * nt * tm
    rmap = lambda b, i: (b * nt + i, 0)
    cmap = lambda b, i: (0, b * nt + i)
    return pl.pallas_call(
        functools.partial(_prep_kernel, tm=tm, valid_rows=valid_rows),
        grid=(batch, nt),
        in_specs=[
            pl.BlockSpec((tm, 3 * DN_WIDTH), rmap),
            pl.BlockSpec((None, None, 16, 3 * DN_WIDTH), lambda b, i: (b, i, 0, 0)),
            pl.BlockSpec((tm, LANE), rmap),
            pl.BlockSpec((8, 3 * DN_WIDTH), lambda b, i: (0, 0)),
            pl.BlockSpec((8, LANE), lambda b, i: (0, 0)),
        ],
        out_specs=[
            pl.BlockSpec((tm, 3 * DN_WIDTH), rmap),
            pl.BlockSpec((DN_WIDTH, tm), cmap),
            pl.BlockSpec((tm, LANE), rmap),
            pl.BlockSpec((GT_ROWS, tm), cmap),
        ],
        out_shape=[
            jax.ShapeDtypeStruct((rows, 3 * DN_WIDTH), BF16),
            jax.ShapeDtypeStruct((DN_WIDTH, rows), BF16),
            jax.ShapeDtypeStruct((rows, LANE), F32),
            jax.ShapeDtypeStruct((GT_ROWS, rows), F32),
        ],
        compiler_params=pltpu.CompilerParams(
            dimension_semantics=("parallel", "parallel"), vmem_limit_bytes=VMEM_LIMIT),
        name="prep",
    )(p, halo, ba, conv_w8, gpar)


def _bd(y):
    yb = y.astype(BF16)
    z = jnp.zeros((HEAD_DIM, HEAD_DIM), BF16)
    top = jnp.concatenate([yb[:, :HEAD_DIM], z], axis=1)
    bot = jnp.concatenate([z, yb[:, HEAD_DIM:]], axis=1)
    return jnp.concatenate([top, bot], axis=0)


def _pair_cols(bg, lane0):
    a = jnp.broadcast_to(bg[:, lane0:lane0 + 1], (CHUNK, HEAD_DIM))
    b = jnp.broadcast_to(bg[:, lane0 + 1:lane0 + 2], (CHUNK, HEAD_DIM))
    return jnp.concatenate([a, b], axis=1)


def _pair_rows(gt, row0, rs):
    return jnp.concatenate([gt[row0:row0 + 1, rs], gt[row0 + 1:row0 + 2, rs]], axis=1)


def _delta_kernel(qf_ref, kf_ref, vf_ref, ktf_ref, bgf_ref, gtf_ref,
                  qb_ref, kb_ref, vb_ref, ktb_ref, bgb_ref, gtb_ref, s0_ref,
                  of_ref, ob_ref, sfin_ref,
                  s_scr, pk_scr, kq_scr, bv_scr, gl_scr, *, nchunk):
    i = pl.program_id(1)

    @pl.when(i == 0)
    def _():
        s_scr[0] = s0_ref[...]
        s_scr[1] = jnp.zeros_like(s_scr[1])

    row = lax.broadcasted_iota(jnp.int32, (CHUNK, PAIR), 0)
    col = lax.broadcasted_iota(jnp.int32, (CHUNK, PAIR), 1) % HEAD_DIM
    eye2 = jnp.where(row == col, 1.0, 0.0).astype(F32)
    dirs = (
        dict(q=qf_ref, k=kf_ref, v=vf_ref, kt=ktf_ref, bg=bgf_ref, gt=gtf_ref, o=of_ref, dcol=0,
             incl=row >= col, strict=row > col, order=tuple(range(nchunk))),
        dict(q=qb_ref, k=kb_ref, v=vb_ref, kt=ktb_ref, bg=bgb_ref, gt=gtb_ref, o=ob_ref, dcol=HEADS,
             incl=row <= col, strict=row < col, order=tuple(range(nchunk - 1, -1, -1))),
    )
    units = [(d, step, p) for d in range(2) for step in range(nchunk) for p in range(NPAIR)]
    steps = int(math.log2(INV_BASE)) - 1
    same_block = {}
    size = INV_BASE
    while size < CHUNK:
        same_block[size] = (row // size) == (col // size)
        size *= 2

    def phase_a(group):
        st = []
        for u, (d, step, p) in group:
            dr = dirs[d]
            c = dr["order"][step]
            rs = slice(c * CHUNK, (c + 1) * CHUNK)
            ls = slice(p * PAIR, (p + 1) * PAIR)
            j0 = dr["dcol"] + 2 * p
            bg = dr["bg"][rs, :]
            gt = dr["gt"]
            beta = _pair_cols(bg, BG_BETA + j0)
            gc = _pair_cols(bg, BG_GC + j0)
            tot = _pair_cols(bg, BG_TOT + j0)
            gr = _pair_rows(gt, BG_GC + j0, rs)
            totr = _pair_rows(gt, BG_TOT + j0, rs)
            decay = jnp.exp(jnp.where(dr["incl"], gc - gr, -1e30))
            kt2 = jnp.concatenate([dr["kt"][p * PAIR:p * PAIR + HEAD_DIM, rs],
                                   dr["kt"][p * PAIR + HEAD_DIM:(p + 1) * PAIR, rs]], axis=1)
            qb = dr["q"][rs, ls]
            kb = dr["k"][rs, ls]
            qk = jnp.dot(jnp.concatenate([qb, kb], axis=0), _bd(kt2), preferred_element_type=F32)
            lm = jnp.where(dr["strict"], beta * qk[CHUNK:] * decay, 0.0)
            egc = jnp.exp(gc)
            kq_scr[u] = jnp.concatenate([(beta * egc) * kb.astype(F32), egc * qb.astype(F32)],
                                        axis=0).astype(BF16)
            bv_scr[u] = beta * dr["v"][rs, ls].astype(F32)
            gl_scr[u] = jnp.exp(tot[0:8, :])
            st.append(dict(lm=lm, pm=qk[:CHUNK] * decay, kdt=kt2.astype(F32) * jnp.exp(totr - gr)))
        for s in st:
            s["x"] = -jnp.where(same_block[INV_BASE], s["lm"], 0.0)
            s["t"] = eye2 + s["x"]
        for s in st:
            s["p"] = _mm(s["x"], _bd(s["x"]))
        for it in range(steps):
            for s in st:
                s["t"] = s["t"] + _mm(s["t"], _bd(s["p"]))
            if it + 1 < steps:
                for s in st:
                    s["p"] = _mm(s["p"], _bd(s["p"]))
        size = INV_BASE
        while size < CHUNK:
            join = ~same_block[size] if 2 * size == CHUNK else same_block[2 * size] & ~same_block[size]
            for s in st:
                s["a"] = _mm(jnp.where(join, s["lm"], 0.0), _bd(s["t"]))
            for s in st:
                s["t"] = s["t"] - _mm(s["t"], _bd(s["a"]))
            size *= 2
        for (u, _), s in zip(group, st):
            pk_scr[u] = _mm(jnp.concatenate([s["pm"], s["kdt"]], axis=0), _bd(s["t"])).astype(BF16)

    indexed = list(enumerate(units))
    for d in range(2):
        phase_a([(u, un) for u, un in indexed if un[0] == d])

    for step in range(nchunk):
        active = [(u, un) for u, un in indexed if un[1] == step]
        xs = [jnp.dot(kq_scr[u], _bd(s_scr[d, p]), preferred_element_type=F32) for u, (d, _, p) in active]
        rr = [bv_scr[u] - x[:CHUNK] for (u, _), x in zip(active, xs)]
        for (u, (d, _, p)), x, r in zip(active, xs, rr):
            dr = dirs[d]
            c = dr["order"][step]
            res = jnp.dot(pk_scr[u], _bd(r), preferred_element_type=F32)
            dr["o"][c * CHUNK:(c + 1) * CHUNK, p * PAIR:(p + 1) * PAIR] = (
                x[CHUNK:] + res[:CHUNK]).astype(dr["o"].dtype)
            s_scr[d, p] = gl_scr[u][0:1, :] * s_scr[d, p] + res[CHUNK:]

    @pl.when(i == pl.num_programs(1) - 1)
    def _():
        sfin_ref[...] = s_scr[0]


def _delta(qkv, kt, bgc, gt, s0, *, batch, nblk, tb):
    rows = batch * nblk * tb
    assert tb % CHUNK == 0 and CHUNK == HEAD_DIM
    nchunk = tb // CHUNK
    nunit = 2 * nchunk * NPAIR
    fwd = lambda b, i: b * nblk + i
    bwd = lambda b, i: b * nblk + (nblk - 1 - i)

    def dir_specs(blk):
        return [
            pl.BlockSpec((tb, DN_WIDTH), lambda b, i: (blk(b, i), 0)),
            pl.BlockSpec((tb, DN_WIDTH), lambda b, i: (blk(b, i), 1)),
            pl.BlockSpec((tb, DN_WIDTH), lambda b, i: (blk(b, i), 2)),
            pl.BlockSpec((DN_WIDTH, tb), lambda b, i: (0, blk(b, i))),
            pl.BlockSpec((tb, LANE), lambda b, i: (blk(b, i), 0)),
            pl.BlockSpec((GT_ROWS, tb), lambda b, i: (0, blk(b, i))),
        ]

    sspec = pl.BlockSpec((None, NPAIR, HEAD_DIM, PAIR), lambda b, i: (b, 0, 0, 0))
    return pl.pallas_call(
        functools.partial(_delta_kernel, nchunk=nchunk),
        grid=(batch, nblk),
        in_specs=dir_specs(fwd) + dir_specs(bwd) + [sspec],
        out_specs=[
            pl.BlockSpec((tb, DN_WIDTH), lambda b, i: (fwd(b, i), 0)),
            pl.BlockSpec((tb, DN_WIDTH), lambda b, i: (bwd(b, i), 0)),
            sspec,
        ],
        out_shape=[
            jax.ShapeDtypeStruct((rows, DN_WIDTH), BF16),
            jax.ShapeDtypeStruct((rows, DN_WIDTH), BF16),
            jax.ShapeDtypeStruct((batch, NPAIR, HEAD_DIM, PAIR), F32),
        ],
        scratch_shapes=[
            pltpu.VMEM((2, NPAIR, HEAD_DIM, PAIR), F32),
            pltpu.VMEM((nunit, CHUNK + HEAD_DIM, PAIR), BF16),
            pltpu.VMEM((nunit, 2 * CHUNK, PAIR), BF16),
            pltpu.VMEM((nunit, CHUNK, PAIR), F32),
            pltpu.VMEM((nunit, 8, PAIR), F32),
        ],
        compiler_params=pltpu.CompilerParams(
            dimension_semantics=("parallel", "arbitrary"), vmem_limit_bytes=VMEM_LIMIT),
        name="delta",
    )(qkv, qkv, qkv, kt, bgc, gt, qkv, qkv, qkv, kt, bgc, gt, s0)


def _phase_tables(num, den):
    ang = 2.0 * np.pi * (num % den).astype(np.float64) / den
    return np.cos(ang), np.sin(ang)


def _fft_factor(t):
    if t <= 4096:
        return t, 1
    best = None
    for n1 in range(16, t, 16):
        if t % n1:
            continue
        n2 = t // n1
        n2p = -(-n2 // T2_GROUP) * T2_GROUP
        cost = 4 * n1 + 2 * n2p
        if best is None or cost < best[0]:
            best = (cost, n1, n2)
    assert best is not None, t
    return best[1], best[2]


def _caxis_kernel(f_ref, cs_ref, a_ref, b_ref):
    cs = cs_ref[...]
    for g in range(FN_GROUPS):
        sl = slice(g * FN_GROUP_DIM, (g + 1) * FN_GROUP_DIM)
        ab = jnp.dot(f_ref[:, sl], cs, preferred_element_type=F32)
        a_ref[:, sl] = ab[:, :FN_GROUP_DIM].astype(a_ref.dtype)
        b_ref[:, sl] = ab[:, FN_GROUP_DIM:].astype(b_ref.dtype)


def _caxis(p, cs):
    rows = p.shape[0]
    tr = _row_tile(rows, 1024)
    spec = pl.BlockSpec((tr, FN_WIDTH), lambda i: (i, 0))
    return pl.pallas_call(
        _caxis_kernel,
        grid=(rows // tr,),
        in_specs=[pl.BlockSpec((tr, FN_WIDTH), lambda i: (i, PCOL_F)),
                  pl.BlockSpec((FN_GROUP_DIM, 2 * FN_GROUP_DIM), lambda i: (0, 0))],
        out_specs=[spec, spec],
        out_shape=[jax.ShapeDtypeStruct((rows, FN_WIDTH), BF16)] * 2,
        compiler_params=pltpu.CompilerParams(
            dimension_semantics=("parallel",), vmem_limit_bytes=VMEM_LIMIT),
        name="dft_channel",
    )(p, cs)


def _dft_dense_kernel(ax_ref, bx_ref, am_ref, bm_ref, cx_ref, sx_ref, cm_ref, sm_ref, o_ref, *, scale):
    dot = functools.partial(jnp.dot, preferred_element_type=F32)
    y = (dot(cx_ref[...], ax_ref[...]) - dot(sx_ref[...], bx_ref[...])
         + dot(cm_ref[...], am_ref[...]) - dot(sm_ref[...], bm_ref[...]))
    o_ref[...] = (y * scale).astype(o_ref.dtype)


def _dft_dense(a_x, b_x, a_m, b_m, *, batch, seq):
    t = N_META + seq
    scale = 1.0 / math.sqrt(t * FN_GROUP_DIM)
    k = np.arange(seq, dtype=np.int64)[:, None] + N_META
    cx, sx = _phase_tables(k * (np.arange(seq, dtype=np.int64)[None, :] + N_META), t)
    cm, sm = _phase_tables(k * np.arange(N_META, dtype=np.int64)[None, :], t)
    tmi = _row_tile(seq, 512)
    nm = seq // tmi
    xspec = pl.BlockSpec((seq, FN_WIDTH), lambda bb, m: (bb, 0))
    mspec = pl.BlockSpec((N_META, FN_WIDTH), lambda bb, m: (0, 0))
    return pl.pallas_call(
        functools.partial(_dft_dense_kernel, scale=scale),
        grid=(batch, nm),
        in_specs=[xspec, xspec, mspec, mspec,
                  pl.BlockSpec((tmi, seq), lambda bb, m: (m, 0)),
                  pl.BlockSpec((tmi, seq), lambda bb, m: (m, 0)),
                  pl.BlockSpec((tmi, N_META), lambda bb, m: (m, 0)),
                  pl.BlockSpec((tmi, N_META), lambda bb, m: (m, 0))],
        out_specs=pl.BlockSpec((tmi, FN_WIDTH), lambda bb, m: (bb * nm + m, 0)),
        out_shape=jax.ShapeDtypeStruct((batch * seq, FN_WIDTH), BF16),
        compiler_params=pltpu.CompilerParams(
            dimension_semantics=("parallel", "parallel"), vmem_limit_bytes=VMEM_LIMIT),
        name="dft_time_dense",
    )(a_x, b_x, a_m, b_m, jnp.asarray(cx, BF16), jnp.asarray(sx, BF16), jnp.asarray(cm, BF16), jnp.asarray(sm, BF16))


def _dft1_kernel(px_ref, pm_ref, cs_ref, c_ref, s_ref, twr_ref, twi_ref, yre_ref, yim_ref, scr, *, t, n1, n2):
    j = pl.program_id(1)
    halves = FN_GROUP_DIM // LANE

    @pl.when(j == 0)
    def _():
        for c in range(halves):
            ls = slice(c * LANE, (c + 1) * LANE)
            scr[c, 0:N_META, :] = pm_ref[:, ls].astype(F32)
            scr[c, N_META:t, :] = px_ref[:, ls].astype(F32)
            scr[c, t:t + T2_GROUP, :] = jnp.zeros((T2_GROUP, LANE), F32)

    dot = functools.partial(jnp.dot, preferred_element_type=F32)
    cs = cs_ref[...]
    c1 = c_ref[...]
    s1 = s_ref[...]
    xs = [jnp.concatenate([scr[c, pl.ds(j * T2_GROUP + tt, n1, stride=n2), :] for c in range(halves)],
                          axis=1).astype(BF16) for tt in range(T2_GROUP)]
    abs_ = [dot(x, cs) for x in xs]
    aa = [ab[:, :FN_GROUP_DIM].astype(BF16) for ab in abs_]
    bb = [ab[:, FN_GROUP_DIM:].astype(BF16) for ab in abs_]
    yre = [dot(c1, a) - dot(s1, b) for a, b in zip(aa, bb)]
    yim = [-(dot(c1, b) + dot(s1, a)) for a, b in zip(aa, bb)]
    for tt in range(T2_GROUP):
        live = j * T2_GROUP + tt < n2
        wr = twr_ref[tt]
        wi = twi_ref[tt]
        for c in range(halves):
            ls = slice(c * LANE, (c + 1) * LANE)
            re = yre[tt][:, ls]
            im = yim[tt][:, ls]
            yre_ref[tt, :, ls] = jnp.where(live, re * wr - im * wi, 0.0).astype(yre_ref.dtype)
            yim_ref[tt, :, ls] = jnp.where(live, re * wi + im * wr, 0.0).astype(yim_ref.dtype)


def _dft2_kernel(yre_ref, yim_ref, c_ref, s_ref, o_ref, re_scr, im_scr, o_scr, *, n2, n2p, scale):
    ncol = o_ref.shape[-1] // LANE

    def fill(i2, carry):
        rows = pl.ds(pl.multiple_of(i2 * 16, 16), 16)
        for c in range(ncol):
            ls = slice(c * LANE, (c + 1) * LANE)
            re_scr[c, rows, :] = yre_ref[i2, :, ls].astype(F32)
            im_scr[c, rows, :] = yim_ref[i2, :, ls].astype(F32)
        return carry

    lax.fori_loop(0, n2p, fill, 0)
    dot = functools.partial(jnp.dot, preferred_element_type=F32)
    c2 = c_ref[...]
    s2 = s_ref[...]
    for k in range(16):
        gather = lambda scr: jnp.concatenate(
            [scr[c, pl.ds(k, n2p, stride=16), :] for c in range(ncol)], axis=1).astype(BF16)
        y = (dot(c2, gather(re_scr)) + dot(s2, gather(im_scr))) * scale
        for c in range(ncol):
            o_scr[c, pl.ds(k, n2p, stride=16), :] = y[:, c * LANE:(c + 1) * LANE]

    def drain(i2, carry):
        rows = pl.ds(pl.multiple_of(i2 * 16, 16), 16)
        for c in range(ncol):
            o_ref[i2, :, c * LANE:(c + 1) * LANE] = o_scr[c, rows, :].astype(o_ref.dtype)
        return carry

    lax.fori_loop(0, n2, drain, 0)


def _dft_four_step(p, p_meta, cs, *, seq, factor=None):
    t = N_META + seq
    n1, n2 = factor if factor is not None else _fft_factor(t)
    assert n1 * n2 == t and n1 % 16 == 0 and n2 > 1
    n2p = -(-n2 // T2_GROUP) * T2_GROUP
    scale = 1.0 / math.sqrt(t * FN_GROUP_DIM)
    k1 = np.arange(n1, dtype=np.int64) + N_META
    c1, s1 = _phase_tables(k1[:, None] * np.arange(n1, dtype=np.int64)[None, :], n1)
    twc, tws = _phase_tables(np.arange(n2p, dtype=np.int64)[:, None] * k1[None, :], t)
    twr = jnp.asarray(np.broadcast_to(twc[:, :, None], (n2p, n1, LANE)), F32)
    twi = jnp.asarray(np.broadcast_to(-tws[:, :, None], (n2p, n1, LANE)), F32)
    fcol = lambda g, j: (0, PCOL_F * FN_GROUPS + g)
    const = lambda g, j: (0, 0)
    yre, yim = pl.pallas_call(
        functools.partial(_dft1_kernel, t=t, n1=n1, n2=n2),
        grid=(FN_GROUPS, n2p // T2_GROUP),
        in_specs=[
            pl.BlockSpec((seq, FN_GROUP_DIM), fcol),
            pl.BlockSpec((N_META, FN_GROUP_DIM), fcol),
            pl.BlockSpec((FN_GROUP_DIM, 2 * FN_GROUP_DIM), const),
            pl.BlockSpec((n1, n1), const),
            pl.BlockSpec((n1, n1), const),
            pl.BlockSpec((T2_GROUP, n1, LANE), lambda g, j: (j, 0, 0)),
            pl.BlockSpec((T2_GROUP, n1, LANE), lambda g, j: (j, 0, 0)),
        ],
        out_specs=[pl.BlockSpec((T2_GROUP, n1, FN_GROUP_DIM), lambda g, j: (j, 0, g))] * 2,
        out_shape=[jax.ShapeDtypeStruct((n2p, n1, FN_WIDTH), BF16)] * 2,
        scratch_shapes=[pltpu.VMEM((FN_GROUP_DIM // LANE, t + T2_GROUP, LANE), F32)],
        compiler_params=pltpu.CompilerParams(
            dimension_semantics=("parallel", "arbitrary"), vmem_limit_bytes=VMEM_LIMIT),
        name="dft_time_1",
    )(p, p_meta, cs, jnp.asarray(c1, BF16), jnp.asarray(s1, BF16), twr, twi)

    c2 = np.zeros((n2p, n2p))
    s2 = np.zeros((n2p, n2p))
    k2 = np.arange(n2, dtype=np.int64)
    c2[:n2, :n2], s2[:n2, :n2] = _phase_tables(k2[:, None] * k2[None, :], n2)
    tn = 512
    out = pl.pallas_call(
        functools.partial(_dft2_kernel, n2=n2, n2p=n2p, scale=scale),
        grid=(n1 // 16, FN_WIDTH // tn),
        in_specs=[
            pl.BlockSpec((n2p, 16, tn), lambda i, h: (0, i, h)),
            pl.BlockSpec((n2p, 16, tn), lambda i, h: (0, i, h)),
            pl.BlockSpec((n2p, n2p), lambda i, h: (0, 0)),
            pl.BlockSpec((n2p, n2p), lambda i, h: (0, 0)),
        ],
        out_specs=pl.BlockSpec((n2, 16, tn), lambda i, h: (0, i, h)),
        out_shape=jax.ShapeDtypeStruct((n2, n1, FN_WIDTH), BF16),
        scratch_shapes=[pltpu.VMEM((tn // LANE, n2p * 16, LANE), F32)] * 3,
        compiler_params=pltpu.CompilerParams(
            dimension_semantics=("parallel", "parallel"), vmem_limit_bytes=VMEM_LIMIT),
        name="dft_time_2",
    )(yre, yim, jnp.asarray(c2, BF16), jnp.asarray(s2, BF16))
    return out.reshape(t, FN_WIDTH)


def _merge_kernel(x_ref, hb_ref, st_ref, of_ref, ob_ref, za_ref, zf_ref, fr_ref, lig_ref, lib_ref, nw_ref,
                  wga_ref, wgf_ref, bga_ref, bgf_ref, wa_ref, wf_ref, wo_ref, lg_ref, lb_ref,
                  y_ref, ya_ref, yf_ref, acc_ref):
    j = pl.program_id(1)

    @pl.when(j == 0)
    def _():
        nw = nw_ref[...]
        for h in range(HEADS):
            hs = slice(h * HEAD_DIM, (h + 1) * HEAD_DIM)
            o = of_ref[:, hs].astype(F32) + ob_ref[:, hs].astype(F32)
            o = o * lax.rsqrt(jnp.mean(o * o, axis=-1, keepdims=True) + RMS_EPS) * nw
            ya_ref[:, hs] = (o * _silu(za_ref[:, hs].astype(F32))).astype(BF16)
        yf_ref[...] = (fr_ref[...].astype(F32) * _silu(zf_ref[...].astype(F32))).astype(BF16)
        acc_ref[...] = jnp.zeros_like(acc_ref)

    hb = hb_ref[...]
    ga = jax.nn.sigmoid(jnp.dot(hb, wga_ref[...], preferred_element_type=F32) + bga_ref[...])
    gf = jax.nn.sigmoid(jnp.dot(hb, wgf_ref[...], preferred_element_type=F32) + bgf_ref[...])
    m = (ga * jnp.dot(ya_ref[...], wa_ref[...], preferred_element_type=F32)
         + gf * jnp.dot(yf_ref[...], wf_ref[...], preferred_element_type=F32))
    acc_ref[...] += jnp.dot(m.astype(BF16), wo_ref[...], preferred_element_type=F32)

    @pl.when(j == pl.num_programs(1) - 1)
    def _():
        st = st_ref[...]
        h = (x_ref[...] - st[:, BA_MU:BA_MU + 1]) * st[:, BA_RSTD:BA_RSTD + 1] * lig_ref[...] + lib_ref[...]
        y_ref[...] = _layer_norm(DEEPNORM_ALPHA * h + acc_ref[...], lg_ref[...], lb_ref[...])


def _merge(x, hb, ba, o_f, o_b, p, fr, ln_in_g, ln_in_b, nw, w_g, b_g, w_a, w_f, w_o, ln_g, ln_b, *,
           tm=512, tn=256):
    rows = x.shape[0]
    assert rows % tm == 0 and D_MODEL % tn == 0
    nj = D_MODEL // tn
    row_spec = lambda w, col=0: pl.BlockSpec((tm, w), lambda i, j: (i, col))
    vec_spec = lambda w: pl.BlockSpec((1, w), lambda i, j: (0, 0))
    return pl.pallas_call(
        _merge_kernel,
        grid=(rows // tm, nj),
        in_specs=[
            row_spec(D_MODEL), row_spec(D_MODEL), row_spec(LANE),
            row_spec(DN_WIDTH), row_spec(DN_WIDTH),
            row_spec(DN_WIDTH, PCOL_ZA), row_spec(FN_WIDTH, PCOL_ZF),
            row_spec(FN_WIDTH),
            vec_spec(D_MODEL), vec_spec(D_MODEL), vec_spec(HEAD_DIM),
            pl.BlockSpec((D_MODEL, tn), lambda i, j: (0, j)),
            pl.BlockSpec((D_MODEL, tn), lambda i, j: (0, nj + j)),
            pl.BlockSpec((1, tn), lambda i, j: (0, j)),
            pl.BlockSpec((1, tn), lambda i, j: (0, nj + j)),
            pl.BlockSpec((DN_WIDTH, tn), lambda i, j: (0, j)),
            pl.BlockSpec((FN_WIDTH, tn), lambda i, j: (0, j)),
            pl.BlockSpec((tn, D_MODEL), lambda i, j: (j, 0)),
            vec_spec(D_MODEL), vec_spec(D_MODEL),
        ],
        out_specs=pl.BlockSpec((tm, D_MODEL), lambda i, j: (i, 0)),
        out_shape=jax.ShapeDtypeStruct((rows, D_MODEL), F32),
        scratch_shapes=[
            pltpu.VMEM((tm, DN_WIDTH), BF16),
            pltpu.VMEM((tm, FN_WIDTH), BF16),
            pltpu.VMEM((tm, D_MODEL), F32),
        ],
        compiler_params=pltpu.CompilerParams(
            dimension_semantics=("parallel", "arbitrary"), vmem_limit_bytes=VMEM_LIMIT),
        name="merge",
    )(x, hb, ba, o_f, o_b, p, p, fr, ln_in_g, ln_in_b, nw, w_g, w_g, b_g, b_g, w_a, w_f, w_o, ln_g, ln_b)


def _encode(x, p_meta, ba_meta, ab_meta, wts, *, fft_factor=None, tm_rows=1024):
    batch, seq, _ = x.shape
    rows = batch * seq
    t_full = N_META + seq
    x2 = x.reshape(rows, D_MODEL)

    p, ba, hb = _ln_inproj(x2, wts["ln_in_g"], wts["ln_in_b"], wts["w_p"], wts["w_ba"],
                       tm=_row_tile(rows, tm_rows))

    tb = DELTA_BLOCK
    assert seq % tb == 0 and seq >= META_BLOCK
    nt = seq // tb
    p4 = p.reshape(batch, nt, tb, P_WIDTH)
    tail = p4[:, :, tb - 2:, :3 * DN_WIDTH]
    head = p4[:, :, :2, :3 * DN_WIDTH]
    meta_tail = jnp.broadcast_to(p_meta[None, None, N_META - 2:, :3 * DN_WIDTH], (batch, 1, 2, 3 * DN_WIDTH))
    prev = jnp.concatenate([meta_tail, tail[:, :-1]], axis=1)
    nxt = jnp.concatenate([head[:, 1:], jnp.zeros((batch, 1, 2, 3 * DN_WIDTH), BF16)], axis=1)
    pad6 = jnp.zeros((batch, nt, 6, 3 * DN_WIDTH), BF16)
    halo = jnp.concatenate([pad6, prev, nxt, pad6], axis=2)
    qkv, kt, bgc, gt = _prep(p, halo, ba, wts["conv_w8"], wts["gpar"], batch=batch, nt=nt, tm=tb)

    mb = META_BLOCK
    p_m = jnp.concatenate([
        jnp.broadcast_to(p_meta[None, :, :3 * DN_WIDTH], (batch, N_META, 3 * DN_WIDTH)),
        p.reshape(batch, seq, P_WIDTH)[:, :mb - N_META, :3 * DN_WIDTH]], axis=1).reshape(batch * mb, 3 * DN_WIDTH)
    ba_m = jnp.concatenate([
        jnp.broadcast_to(ba_meta[None], (batch, N_META, LANE)),
        ba.reshape(batch, seq, LANE)[:, :mb - N_META]], axis=1).reshape(batch * mb, LANE)
    halo_m = jnp.zeros((batch, 1, 16, 3 * DN_WIDTH), BF16)
    qkv_m, kt_m, bgc_m, gt_m = _prep(p_m, halo_m, ba_m, wts["conv_w8"], wts["gpar"],
                                     batch=batch, nt=1, tm=mb, valid_rows=N_META)
    s_zero = jnp.zeros((batch, NPAIR, HEAD_DIM, PAIR), F32)
    _, _, s_meta = _delta(qkv_m, kt_m, bgc_m, gt_m, s_zero, batch=batch, nblk=1, tb=mb)

    o_f, o_b, _ = _delta(qkv, kt, bgc, gt, s_meta, batch=batch, nblk=nt, tb=tb)

    n1, n2 = fft_factor if fft_factor is not None else _fft_factor(t_full)
    if n2 == 1:
        a_x, b_x = _caxis(p, wts["cs"])
        fr = _dft_dense(a_x, b_x, ab_meta[0], ab_meta[1], batch=batch, seq=seq)
    else:
        assert batch == 1
        fr = _dft_four_step(p, p_meta, wts["cs"], seq=seq, factor=(n1, n2))

    y = _merge(x2, hb, ba, o_f, o_b, p, fr, wts["ln_in_g"], wts["ln_in_b"], wts["nw"], wts["w_g"], wts["b_g"],
               wts["w_a"], wts["w_f"], wts["w_o"], wts["ln_g"], wts["ln_b"], tm=_row_tile(rows, 512))
    return y.reshape(batch, seq, D_MODEL)


def _prepare_weights(ln_in_g, ln_in_b, w_in, b_gate, conv_w, a_log, dt_bias, dn_norm_w,
                     w_proj_a, w_proj_f, w_out, ln_g, ln_b):
    w = w_in[0]
    col_b = 4 * DN_WIDTH
    col_f = col_b + 4 * HEADS
    col_g = col_f + 2 * FN_WIDTH
    w_p = jnp.concatenate([w[:, :col_b], w[:, col_f:col_g]], axis=1).astype(BF16)
    w_ba = jnp.pad(w[:, col_b:col_f], ((0, 0), (0, LANE - 4 * HEADS))).astype(BF16)
    gpar = jnp.zeros((8, LANE), F32)
    gpar = gpar.at[0, BG_G:BG_GC].set(a_log[0].reshape(-1).astype(F32))
    gpar = gpar.at[1, BG_G:BG_GC].set(dt_bias[0].reshape(-1).astype(F32))
    k = np.arange(FN_GROUP_DIM, dtype=np.int64)
    ang = 2.0 * np.pi * ((k[:, None] * k[None, :]) % FN_GROUP_DIM) / FN_GROUP_DIM
    cs = jnp.asarray(np.concatenate([np.cos(ang), np.sin(ang)], axis=1), BF16)
    return {
        "ln_in_g": ln_in_g.reshape(1, D_MODEL).astype(F32),
        "ln_in_b": ln_in_b.reshape(1, D_MODEL).astype(F32),
        "w_p": w_p,
        "w_ba": w_ba,
        "conv_w8": jnp.pad(conv_w[0].astype(F32), ((0, 8 - CONV_K), (0, 0))),
        "gpar": gpar,
        "cs": cs,
        "nw": dn_norm_w[0].reshape(1, HEAD_DIM).astype(F32),
        "w_g": w[:, col_g:].astype(BF16),
        "b_g": b_gate[0].reshape(1, 2 * D_MODEL).astype(F32),
        "w_a": w_proj_a[0].astype(BF16),
        "w_f": w_proj_f[0].astype(BF16),
        "w_o": w_out[0].astype(BF16),
        "ln_g": ln_g[0].reshape(1, D_MODEL).astype(F32),
        "ln_b": ln_b[0].reshape(1, D_MODEL).astype(F32),
    }


def kernel(x_prompt, x_sample, meta_tokens, ln_in_g, ln_in_b, w_in, b_gate, conv_w, a_log, dt_bias, dn_norm_w,
           w_proj_a, w_proj_f, w_out, ln_g, ln_b):
    wts = _prepare_weights(ln_in_g, ln_in_b, w_in, b_gate, conv_w, a_log, dt_bias, dn_norm_w,
                           w_proj_a, w_proj_f, w_out, ln_g, ln_b)
    p_meta, ba_meta, _ = _ln_inproj(meta_tokens.astype(F32), wts["ln_in_g"], wts["ln_in_b"],
                                    wts["w_p"], wts["w_ba"], tm=N_META)
    ab_meta = _caxis(p_meta, wts["cs"])
    y_prompt = _encode(x_prompt, p_meta, ba_meta, ab_meta, wts)
    y_sample = _encode(x_sample, p_meta, ba_meta, ab_meta, wts)
    return (y_prompt, y_sample)
```

```python
import functools
import math

import numpy as np
import jax
import jax.numpy as jnp
from jax import lax
from jax.experimental import pallas as pl
from jax.experimental.pallas import tpu as pltpu

F32 = jnp.float32
BF16 = jnp.bfloat16

D_MODEL = 2048
N_META = 16
HEADS = 8
HEAD_DIM = 128
DN_WIDTH = HEADS * HEAD_DIM
CONV_K = 5
FN_GROUPS = 4
FN_GROUP_DIM = 256
FN_WIDTH = FN_GROUPS * FN_GROUP_DIM
DEPTH = 1
DEEPNORM_ALPHA = (2 * DEPTH) ** 0.25
LN_EPS = 1e-5
RMS_EPS = 1e-6
L2_EPS = 1e-6

LANE = 128
CHUNK = HEAD_DIM
PAIR = 2 * HEAD_DIM
NPAIR = HEADS // 2
INV_BASE = 16
DELTA_BLOCK = 256
META_BLOCK = CHUNK
T2_GROUP = 16
VMEM_LIMIT = 56 * 1024 * 1024
VMEM_LIMIT_RESIDENT = 60 * 1024 * 1024
INPROJ_ROWS = 512
MERGE_ROWS = 256

PCOL_Q, PCOL_K, PCOL_V, PCOL_ZA, PCOL_F, PCOL_ZF = range(6)
P_WIDTH = 6 * 1024
BG_BETA, BG_G, BG_GC, BG_TOT = 0, 16, 32, 48
BA_MU, BA_RSTD = 32, 33
GT_ROWS = 64


def _mm(a, b):
    return jnp.dot(a.astype(BF16), b.astype(BF16), preferred_element_type=F32)


def _layer_norm(x, g, b):
    mu = jnp.mean(x, axis=-1, keepdims=True)
    xc = x - mu
    var = jnp.mean(xc * xc, axis=-1, keepdims=True)
    return xc * lax.rsqrt(var + LN_EPS) * g + b


def _silu(x):
    return x * jax.nn.sigmoid(x)


def _row_tile(rows, cap, mult=16):
    best = None
    for t in range(mult, min(rows, cap) + 1, mult):
        if rows % t == 0:
            best = t
    assert best is not None, (rows, cap, mult)
    return best


def _ln_inproj_kernel(x_ref, g_ref, b_ref, w_ref, wba_ref, p_ref, ba_ref, hb_ref, *, tn):
    x = x_ref[...]
    mu = jnp.mean(x, axis=-1, keepdims=True)
    xc = x - mu
    rstd = lax.rsqrt(jnp.mean(xc * xc, axis=-1, keepdims=True) + LN_EPS)
    hb = (xc * rstd * g_ref[...] + b_ref[...]).astype(BF16)
    hb_ref[...] = hb
    lane = lax.broadcasted_iota(jnp.int32, ba_ref.shape, 1)
    ba = jnp.dot(hb, wba_ref[...], preferred_element_type=F32)
    ba_ref[...] = jnp.where(lane == BA_MU, mu, jnp.where(lane == BA_RSTD, rstd, ba))
    for j in range(P_WIDTH // tn):
        cs = slice(j * tn, (j + 1) * tn)
        p_ref[:, cs] = jnp.dot(hb, w_ref[:, cs], preferred_element_type=F32).astype(p_ref.dtype)


def _resident(shape):
    return pl.BlockSpec(shape, lambda i: (0,) * len(shape), pipeline_mode=pl.Buffered(1))


def _ln_inproj(x, ln_g, ln_b, w_p, w_ba, *, tm, tn=1024):
    rows = x.shape[0]
    assert rows % tm == 0 and P_WIDTH % tn == 0
    return pl.pallas_call(
        functools.partial(_ln_inproj_kernel, tn=tn),
        grid=(rows // tm,),
        in_specs=[
            pl.BlockSpec((tm, D_MODEL), lambda i: (i, 0)),
            _resident((1, D_MODEL)),
            _resident((1, D_MODEL)),
            _resident((D_MODEL, P_WIDTH)),
            _resident((D_MODEL, LANE)),
        ],
        out_specs=[
            pl.BlockSpec((tm, P_WIDTH), lambda i: (i, 0)),
            pl.BlockSpec((tm, LANE), lambda i: (i, 0)),
            pl.BlockSpec((tm, D_MODEL), lambda i: (i, 0)),
        ],
        out_shape=[
            jax.ShapeDtypeStruct((rows, P_WIDTH), BF16),
            jax.ShapeDtypeStruct((rows, LANE), F32),
            jax.ShapeDtypeStruct((rows, D_MODEL), BF16),
        ],
        compiler_params=pltpu.CompilerParams(
            dimension_semantics=("parallel",), vmem_limit_bytes=VMEM_LIMIT_RESIDENT),
        name="ln_inproj",
    )(x, ln_g, ln_b, w_p, w_ba)


def _split3(x):
    hi = x.astype(BF16)
    r1 = x - hi.astype(F32)
    mid = r1.astype(BF16)
    lo = (r1 - mid.astype(F32)).astype(BF16)
    return hi, mid, lo


def _prep_kernel(p_ref, halo_ref, ba_ref, cw_ref, gpar_ref, qkv_ref, kt_ref, bgc_ref, gt_ref, *, tm, valid_rows):
    cw = cw_ref[...]
    row = lax.broadcasted_iota(jnp.int32, (tm, 1), 0)
    if valid_rows is not None:
        keep = row < valid_rows

    shifts = (-2, -1, 1, 2)
    r = lax.broadcasted_iota(jnp.int32, (tm, tm), 0)
    c = lax.broadcasted_iota(jnp.int32, (tm, tm), 1)
    rh = lax.broadcasted_iota(jnp.int32, (tm, 16), 0)
    ch = lax.broadcasted_iota(jnp.int32, (tm, 16), 1)
    sel = jnp.concatenate([jnp.where(c == r + s, 1.0, 0.0) for s in shifts], axis=0).astype(BF16)
    sel_halo = jnp.concatenate(
        [jnp.where(((rh + s < 0) & (ch == 8 + rh + s)) | ((rh + s >= tm) & (ch == 8 + rh + s - tm)), 1.0, 0.0)
         for s in shifts], axis=0).astype(BF16)
    def shifted(j):
        ls = slice(j * PAIR, (j + 1) * PAIR)
        return (jnp.dot(sel, p_ref[:, ls], preferred_element_type=F32)
                + jnp.dot(sel_halo, halo_ref[:, ls], preferred_element_type=F32))

    nchunks = 3 * DN_WIDTH // PAIR
    ahead = shifted(0)
    for j in range(nchunks):
        ls = slice(j * PAIR, (j + 1) * PAIR)
        moved = ahead
        if j + 1 < nchunks:
            ahead = shifted(j + 1)
        acc = p_ref[:, ls].astype(F32) * cw[2:3, ls]
        for n, s in enumerate(shifts):
            acc = acc + moved[n * tm:(n + 1) * tm] * cw[2 + s:3 + s, ls]
        act = _silu(acc)
        if valid_rows is not None:
            act = jnp.where(keep, act, 0.0)
        if j * PAIR >= 2 * DN_WIDTH:
            qkv_ref[:, ls] = act.astype(qkv_ref.dtype)
            continue
        scale = HEAD_DIM ** -0.5 if j * PAIR < DN_WIDTH else 1.0
        for half in range(PAIR // HEAD_DIM):
            t = act[:, half * HEAD_DIM:(half + 1) * HEAD_DIM]
            tn = t * (lax.rsqrt(jnp.sum(t * t, axis=-1, keepdims=True) + L2_EPS) * scale)
            col = j * PAIR + half * HEAD_DIM
            qkv_ref[:, col:col + HEAD_DIM] = tn.astype(qkv_ref.dtype)
            if col >= DN_WIDTH:
                kt_ref[col - DN_WIDTH:col - DN_WIDTH + HEAD_DIM, :] = tn.T.astype(kt_ref.dtype)

    ba = ba_ref[...]
    gpar = gpar_ref[...]
    lane = lax.broadcasted_iota(jnp.int32, (tm, LANE), 1)
    beta = jax.nn.sigmoid(ba)
    z = ba + gpar[1:2, :]
    softplus = jnp.maximum(z, 0.0) + jnp.log1p(jnp.exp(-jnp.abs(z)))
    g = -jnp.exp(gpar[0:1, :]) * softplus
    is_g = (lane >= BG_G) & (lane < BG_GC)
    g = jnp.where(is_g, g, 0.0)
    beta = jnp.where(lane < BG_G, beta, 0.0)
    if valid_rows is not None:
        g = jnp.where(keep, g, 0.0)
        beta = jnp.where(keep, beta, 0.0)

    r = lax.broadcasted_iota(jnp.int32, (tm, tm), 0)
    c = lax.broadcasted_iota(jnp.int32, (tm, tm), 1)
    same = (r // CHUNK) == (c // CHUNK)
    m_pre = jnp.where(same & (r >= c), 1.0, 0.0).astype(BF16)
    m_suf = jnp.where(same & (r <= c), 1.0, 0.0).astype(BF16)
    pre = jnp.zeros((tm, LANE), F32)
    suf = jnp.zeros((tm, LANE), F32)
    for part in _split3(g):
        pre = pre + jnp.dot(m_pre, part, preferred_element_type=F32)
        suf = suf + jnp.dot(m_suf, part, preferred_element_type=F32)
    tot = pre + suf - g
    fwd_lane = lane < BG_G + HEADS
    gc = jnp.where(fwd_lane, pre, suf)
    out = beta + g + pltpu.roll(gc, BG_GC - BG_G, axis=1) + pltpu.roll(tot, BG_TOT - BG_G, axis=1)
    bgc_ref[...] = out
    gt_ref[...] = out.T[:GT_ROWS, :]


def _prep(p, halo, ba, conv_w8, gpar, *, batch, nt, tm, valid_rows=None):
    rows = batch * nt * tm
    rmap = lambda b, i: (b * nt + i, 0)
    cmap = lambda b, i: (0, b * nt + i)
    return pl.pallas_call(
        functools.partial(_prep_kernel, tm=tm, valid_rows=valid_rows),
        grid=(batch, nt),
        in_specs=[
            pl.BlockSpec((tm, 3 * DN_WIDTH), rmap),
            pl.BlockSpec((None, None, 16, 3 * DN_WIDTH), lambda b, i: (b, i, 0, 0)),
            pl.BlockSpec((tm, LANE), rmap),
            pl.BlockSpec((8, 3 * DN_WIDTH), lambda b, i: (0, 0)),
            pl.BlockSpec((8, LANE), lambda b, i: (0, 0)),
        ],
        out_specs=[
            pl.BlockSpec((tm, 3 * DN_WIDTH), rmap),
            pl.BlockSpec((DN_WIDTH, tm), cmap),
            pl.BlockSpec((tm, LANE), rmap),
            pl.BlockSpec((GT_ROWS, tm), cmap),
        ],
        out_shape=[
            jax.ShapeDtypeStruct((rows, 3 * DN_WIDTH), BF16),
            jax.ShapeDtypeStruct((DN_WIDTH, rows), BF16),
            jax.ShapeDtypeStruct((rows, LANE), F32),
            jax.ShapeDtypeStruct((GT_ROWS, rows), F32),
        ],
        compiler_params=pltpu.CompilerParams(
            dimension_semantics=("parallel", "parallel"), vmem_limit_bytes=VMEM_LIMIT),
        name="prep",
    )(p, halo, ba, conv_w8, gpar)


def _bd(y):
    yb = y.astype(BF16)
    z = jnp.zeros((HEAD_DIM, HEAD_DIM), BF16)
    top = jnp.concatenate([yb[:, :HEAD_DIM], z], axis=1)
    bot = jnp.concatenate([z, yb[:, HEAD_DIM:]], axis=1)
    return jnp.concatenate([top, bot], axis=0)


def _pair_cols(bg, lane0):
    a = jnp.broadcast_to(bg[:, lane0:lane0 + 1], (CHUNK, HEAD_DIM))
    b = jnp.broadcast_to(bg[:, lane0 + 1:lane0 + 2], (CHUNK, HEAD_DIM))
    return jnp.concatenate([a, b], axis=1)


def _pair_rows(gt, row0, rs):
    return jnp.concatenate([gt[row0:row0 + 1, rs], gt[row0 + 1:row0 + 2, rs]], axis=1)


def _delta_kernel(qf_ref, kf_ref, vf_ref, ktf_ref, bgf_ref, gtf_ref,
                  qb_ref, kb_ref, vb_ref, ktb_ref, bgb_ref, gtb_ref, s0_ref,
                  of_ref, ob_ref, sfin_ref,
                  s_scr, pk_scr, kq_scr, bv_scr, gl_scr, *, nchunk):
    i = pl.program_id(1)

    @pl.when(i == 0)
    def _():
        s_scr[0] = s0_ref[...]
        s_scr[1] = jnp.zeros_like(s_scr[1])

    row = lax.broadcasted_iota(jnp.int32, (CHUNK, PAIR), 0)
    col = lax.broadcasted_iota(jnp.int32, (CHUNK, PAIR), 1) % HEAD_DIM
    eye2 = jnp.where(row == col, 1.0, 0.0).astype(F32)
    dirs = (
        dict(q=qf_ref, k=kf_ref, v=vf_ref, kt=ktf_ref, bg=bgf_ref, gt=gtf_ref, o=of_ref, dcol=0,
             incl=row >= col, strict=row > col, order=tuple(range(nchunk))),
        dict(q=qb_ref, k=kb_ref, v=vb_ref, kt=ktb_ref, bg=bgb_ref, gt=gtb_ref, o=ob_ref, dcol=HEADS,
             incl=row <= col, strict=row < col, order=tuple(range(nchunk - 1, -1, -1))),
    )
    units = [(d, step, p) for d in range(2) for step in range(nchunk) for p in range(NPAIR)]
    steps = int(math.log2(INV_BASE)) - 1
    same_block = {}
    size = INV_BASE
    while size < CHUNK:
        same_block[size] = (row // size) == (col // size)
        size *= 2

    def phase_a(group):
        st = []
        for u, (d, step, p) in group:
            dr = dirs[d]
            c = dr["order"][step]
            rs = slice(c * CHUNK, (c + 1) * CHUNK)
            ls = slice(p * PAIR, (p + 1) * PAIR)
            j0 = dr["dcol"] + 2 * p
            bg = dr["bg"][rs, :]
            gt = dr["gt"]
            beta = _pair_cols(bg, BG_BETA + j0)
            gc = _pair_cols(bg, BG_GC + j0)
            tot = _pair_cols(bg, BG_TOT + j0)
            gr = _pair_rows(gt, BG_GC + j0, rs)
            totr = _pair_rows(gt, BG_TOT + j0, rs)
            decay = jnp.exp(jnp.where(dr["incl"], gc - gr, -1e30))
            kt2 = jnp.concatenate([dr["kt"][p * PAIR:p * PAIR + HEAD_DIM, rs],
                                   dr["kt"][p * PAIR + HEAD_DIM:(p + 1) * PAIR, rs]], axis=1)
            qb = dr["q"][rs, ls]
            kb = dr["k"][rs, ls]
            qk = jnp.dot(jnp.concatenate([qb, kb], axis=0), _bd(kt2), preferred_element_type=F32)
            lm = jnp.where(dr["strict"], beta * qk[CHUNK:] * decay, 0.0)
            egc = jnp.exp(gc)
            kq_scr[u] = jnp.concatenate([(beta * egc) * kb.astype(F32), egc * qb.astype(F32)],
                                        axis=0).astype(BF16)
            bv_scr[u] = beta * dr["v"][rs, ls].astype(F32)
            gl_scr[u] = jnp.exp(tot[0:8, :])
            st.append(dict(lm=lm, pm=qk[:CHUNK] * decay, kdt=kt2.astype(F32) * jnp.exp(totr - gr)))
        for s in st:
            s["x"] = -jnp.where(same_block[INV_BASE], s["lm"], 0.0)
            s["t"] = eye2 + s["x"]
        for s in st:
            s["p"] = _mm(s["x"], _bd(s["x"]))
        for it in range(steps):
            for s in st:
                s["t"] = s["t"] + _mm(s["t"], _bd(s["p"]))
            if it + 1 < steps:
                for s in st:
                    s["p"] = _mm(s["p"], _bd(s["p"]))
        size = INV_BASE
        while size < CHUNK:
            join = ~same_block[size] if 2 * size == CHUNK else same_block[2 * size] & ~same_block[size]
            for s in st:
                s["a"] = _mm(jnp.where(join, s["lm"], 0.0), _bd(s["t"]))
            for s in st:
                s["t"] = s["t"] - _mm(s["t"], _bd(s["a"]))
            size *= 2
        for (u, _), s in zip(group, st):
            pk_scr[u] = _mm(jnp.concatenate([s["pm"], s["kdt"]], axis=0), _bd(s["t"])).astype(BF16)

    indexed = list(enumerate(units))
    for d in range(2):
        phase_a([(u, un) for u, un in indexed if un[0] == d])

    for step in range(nchunk):
        active = [(u, un) for u, un in indexed if un[1] == step]
        xs = [jnp.dot(kq_scr[u], _bd(s_scr[d, p]), preferred_element_type=F32) for u, (d, _, p) in active]
        rr = [bv_scr[u] - x[:CHUNK] for (u, _), x in zip(active, xs)]
        for (u, (d, _, p)), x, r in zip(active, xs, rr):
            dr = dirs[d]
            c = dr["order"][step]
            res = jnp.dot(pk_scr[u], _bd(r), preferred_element_type=F32)
            dr["o"][c * CHUNK:(c + 1) * CHUNK, p * PAIR:(p + 1) * PAIR] = (
                x[CHUNK:] + res[:CHUNK]).astype(dr["o"].dtype)
            s_scr[d, p] = gl_scr[u][0:1, :] * s_scr[d, p] + res[CHUNK:]

    @pl.when(i == pl.num_programs(1) - 1)
    def _():
        sfin_ref[...] = s_scr[0]


def _delta(qkv, kt, bgc, gt, s0, *, batch, nblk, tb):
    rows = batch * nblk * tb
    assert tb % CHUNK == 0 and CHUNK == HEAD_DIM
    nchunk = tb // CHUNK
    nunit = 2 * nchunk * NPAIR
    fwd = lambda b, i: b * nblk + i
    bwd = lambda b, i: b * nblk + (nblk - 1 - i)

    def dir_specs(blk):
        return [
            pl.BlockSpec((tb, DN_WIDTH), lambda b, i: (blk(b, i), 0)),
            pl.BlockSpec((tb, DN_WIDTH), lambda b, i: (blk(b, i), 1)),
            pl.BlockSpec((tb, DN_WIDTH), lambda b, i: (blk(b, i), 2)),
            pl.BlockSpec((DN_WIDTH, tb), lambda b, i: (0, blk(b, i))),
            pl.BlockSpec((tb, LANE), lambda b, i: (blk(b, i), 0)),
            pl.BlockSpec((GT_ROWS, tb), lambda b, i: (0, blk(b, i))),
        ]

    sspec = pl.BlockSpec((None, NPAIR, HEAD_DIM, PAIR), lambda b, i: (b, 0, 0, 0))
    return pl.pallas_call(
        functools.partial(_delta_kernel, nchunk=nchunk),
        grid=(batch, nblk),
        in_specs=dir_specs(fwd) + dir_specs(bwd) + [sspec],
        out_specs=[
            pl.BlockSpec((tb, DN_WIDTH), lambda b, i: (fwd(b, i), 0)),
            pl.BlockSpec((tb, DN_WIDTH), lambda b, i: (bwd(b, i), 0)),
            sspec,
        ],
        out_shape=[
            jax.ShapeDtypeStruct((rows, DN_WIDTH), BF16),
            jax.ShapeDtypeStruct((rows, DN_WIDTH), BF16),
            jax.ShapeDtypeStruct((batch, NPAIR, HEAD_DIM, PAIR), F32),
        ],
        scratch_shapes=[
            pltpu.VMEM((2, NPAIR, HEAD_DIM, PAIR), F32),
            pltpu.VMEM((nunit, CHUNK + HEAD_DIM, PAIR), BF16),
            pltpu.VMEM((nunit, 2 * CHUNK, PAIR), BF16),
            pltpu.VMEM((nunit, CHUNK, PAIR), F32),
            pltpu.VMEM((nunit, 8, PAIR), F32),
        ],
        compiler_params=pltpu.CompilerParams(
            dimension_semantics=("parallel", "arbitrary"), vmem_limit_bytes=VMEM_LIMIT),
        name="delta",
    )(qkv, qkv, qkv, kt, bgc, gt, qkv, qkv, qkv, kt, bgc, gt, s0)


def _phase_tables(num, den):
    ang = 2.0 * np.pi * (num % den).astype(np.float64) / den
    return np.cos(ang), np.sin(ang)


def _fft_factor(t):
    if t <= 4096:
        return t, 1
    best = None
    for n1 in range(16, t, 16):
        if t % n1:
            continue
        n2 = t // n1
        n2p = -(-n2 // T2_GROUP) * T2_GROUP
        cost = 4 * n1 + 2 * n2p
        if best is None or cost < best[0]:
            best = (cost, n1, n2)
    assert best is not None, t
    return best[1], best[2]


def _caxis_kernel(f_ref, cs_ref, a_ref, b_ref):
    cs = cs_ref[...]
    for g in range(FN_GROUPS):
        sl = slice(g * FN_GROUP_DIM, (g + 1) * FN_GROUP_DIM)
        ab = jnp.dot(f_ref[:, sl], cs, preferred_element_type=F32)
        a_ref[:, sl] = ab[:, :FN_GROUP_DIM].astype(a_ref.dtype)
        b_ref[:, sl] = ab[:, FN_GROUP_DIM:].astype(b_ref.dtype)


def _caxis(p, cs):
    rows = p.shape[0]
    tr = _row_tile(rows, 1024)
    spec = pl.BlockSpec((tr, FN_WIDTH), lambda i: (i, 0))
    return pl.pallas_call(
        _caxis_kernel,
        grid=(rows // tr,),
        in_specs=[pl.BlockSpec((tr, FN_WIDTH), lambda i: (i, PCOL_F)),
                  pl.BlockSpec((FN_GROUP_DIM, 2 * FN_GROUP_DIM), lambda i: (0, 0))],
        out_specs=[spec, spec],
        out_shape=[jax.ShapeDtypeStruct((rows, FN_WIDTH), BF16)] * 2,
        compiler_params=pltpu.CompilerParams(
            dimension_semantics=("parallel",), vmem_limit_bytes=VMEM_LIMIT),
        name="dft_channel",
    )(p, cs)


def _dft_dense_kernel(ax_ref, bx_ref, am_ref, bm_ref, cx_ref, sx_ref, cm_ref, sm_ref, o_ref, *, scale):
    dot = functools.partial(jnp.dot, preferred_element_type=F32)
    y = (dot(cx_ref[...], ax_ref[...]) - dot(sx_ref[...], bx_ref[...])
         + dot(cm_ref[...], am_ref[...]) - dot(sm_ref[...], bm_ref[...]))
    o_ref[...] = (y * scale).astype(o_ref.dtype)


def _dft_dense(a_x, b_x, a_m, b_m, *, batch, seq):
    t = N_META + seq
    scale = 1.0 / math.sqrt(t * FN_GROUP_DIM)
    k = np.arange(seq, dtype=np.int64)[:, None] + N_META
    cx, sx = _phase_tables(k * (np.arange(seq, dtype=np.int64)[None, :] + N_META), t)
    cm, sm = _phase_tables(k * np.arange(N_META, dtype=np.int64)[None, :], t)
    tmi = _row_tile(seq, 512)
    nm = seq // tmi
    xspec = pl.BlockSpec((seq, FN_WIDTH), lambda bb, m: (bb, 0))
    mspec = pl.BlockSpec((N_META, FN_WIDTH), lambda bb, m: (0, 0))
    return pl.pallas_call(
        functools.partial(_dft_dense_kernel, scale=scale),
        grid=(batch, nm),
        in_specs=[xspec, xspec, mspec, mspec,
                  pl.BlockSpec((tmi, seq), lambda bb, m: (m, 0)),
                  pl.BlockSpec((tmi, seq), lambda bb, m: (m, 0)),
                  pl.BlockSpec((tmi, N_META), lambda bb, m: (m, 0)),
                  pl.BlockSpec((tmi, N_META), lambda bb, m: (m, 0))],
        out_specs=pl.BlockSpec((tmi, FN_WIDTH), lambda bb, m: (bb * nm + m, 0)),
        out_shape=jax.ShapeDtypeStruct((batch * seq, FN_WIDTH), BF16),
        compiler_params=pltpu.CompilerParams(
            dimension_semantics=("parallel", "parallel"), vmem_limit_bytes=VMEM_LIMIT),
        name="dft_time_dense",
    )(a_x, b_x, a_m, b_m, jnp.asarray(cx, BF16), jnp.asarray(sx, BF16), jnp.asarray(cm, BF16), jnp.asarray(sm, BF16))


def _dft1_kernel(px_ref, pm_ref, cs_ref, c_ref, s_ref, twr_ref, twi_ref, yre_ref, yim_ref, scr, *, t, n1, n2):
    j = pl.program_id(1)
    halves = FN_GROUP_DIM // LANE

    @pl.when(j == 0)
    def _():
        for c in range(halves):
            ls = slice(c * LANE, (c + 1) * LANE)
            scr[c, 0:N_META, :] = pm_ref[:, ls].astype(F32)
            scr[c, N_META:t, :] = px_ref[:, ls].astype(F32)
            scr[c, t:t + T2_GROUP, :] = jnp.zeros((T2_GROUP, LANE), F32)

    dot = functools.partial(jnp.dot, preferred_element_type=F32)
    cs = cs_ref[...]
    c1 = c_ref[...]
    s1 = s_ref[...]
    xs = [jnp.concatenate([scr[c, pl.ds(j * T2_GROUP + tt, n1, stride=n2), :] for c in range(halves)],
                          axis=1).astype(BF16) for tt in range(T2_GROUP)]
    abs_ = [dot(x, cs) for x in xs]
    aa = [ab[:, :FN_GROUP_DIM].astype(BF16) for ab in abs_]
    bb = [ab[:, FN_GROUP_DIM:].astype(BF16) for ab in abs_]
    yre = [dot(c1, a) - dot(s1, b) for a, b in zip(aa, bb)]
    yim = [-(dot(c1, b) + dot(s1, a)) for a, b in zip(aa, bb)]
    for tt in range(T2_GROUP):
        live = j * T2_GROUP + tt < n2
        wr = twr_ref[tt]
        wi = twi_ref[tt]
        for c in range(halves):
            ls = slice(c * LANE, (c + 1) * LANE)
            re = yre[tt][:, ls]
            im = yim[tt][:, ls]
            yre_ref[tt, :, ls] = jnp.where(live, re * wr - im * wi, 0.0).astype(yre_ref.dtype)
            yim_ref[tt, :, ls] = jnp.where(live, re * wi + im * wr, 0.0).astype(yim_ref.dtype)


def _dft2_kernel(yre_ref, yim_ref, c_ref, s_ref, o_ref, re_scr, im_scr, o_scr, *, n2, n2p, scale):
    ncol = o_ref.shape[-1] // LANE

    def fill(i2, carry):
        rows = pl.ds(pl.multiple_of(i2 * 16, 16), 16)
        for c in range(ncol):
            ls = slice(c * LANE, (c + 1) * LANE)
            re_scr[c, rows, :] = yre_ref[i2, :, ls].astype(F32)
            im_scr[c, rows, :] = yim_ref[i2, :, ls].astype(F32)
        return carry

    lax.fori_loop(0, n2p, fill, 0)
    dot = functools.partial(jnp.dot, preferred_element_type=F32)
    c2 = c_ref[...]
    s2 = s_ref[...]
    for k in range(16):
        gather = lambda scr: jnp.concatenate(
            [scr[c, pl.ds(k, n2p, stride=16), :] for c in range(ncol)], axis=1).astype(BF16)
        y = (dot(c2, gather(re_scr)) + dot(s2, gather(im_scr))) * scale
        for c in range(ncol):
            o_scr[c, pl.ds(k, n2p, stride=16), :] = y[:, c * LANE:(c + 1) * LANE]

    def drain(i2, carry):
        rows = pl.ds(pl.multiple_of(i2 * 16, 16), 16)
        for c in range(ncol):
            o_ref[i2, :, c * LANE:(c + 1) * LANE] = o_scr[c, rows, :].astype(o_ref.dtype)
        return carry

    lax.fori_loop(0, n2, drain, 0)


def _dft_four_step(p, p_meta, cs, *, seq, factor=None):
    t = N_META + seq
    n1, n2 = factor if factor is not None else _fft_factor(t)
    assert n1 * n2 == t and n1 % 16 == 0 and n2 > 1
    n2p = -(-n2 // T2_GROUP) * T2_GROUP
    scale = 1.0 / math.sqrt(t * FN_GROUP_DIM)
    k1 = np.arange(n1, dtype=np.int64) + N_META
    c1, s1 = _phase_tables(k1[:, None] * np.arange(n1, dtype=np.int64)[None, :], n1)
    twc, tws = _phase_tables(np.arange(n2p, dtype=np.int64)[:, None] * k1[None, :], t)
    twr = jnp.asarray(np.broadcast_to(twc[:, :, None], (n2p, n1, LANE)), F32)
    twi = jnp.asarray(np.broadcast_to(-tws[:, :, None], (n2p, n1, LANE)), F32)
    fcol = lambda g, j: (0, PCOL_F * FN_GROUPS + g)
    const = lambda g, j: (0, 0)
    yre, yim = pl.pallas_call(
        functools.partial(_dft1_kernel, t=t, n1=n1, n2=n2),
        grid=(FN_GROUPS, n2p // T2_GROUP),
        in_specs=[
            pl.BlockSpec((seq, FN_GROUP_DIM), fcol),
            pl.BlockSpec((N_META, FN_GROUP_DIM), fcol),
            pl.BlockSpec((FN_GROUP_DIM, 2 * FN_GROUP_DIM), const),
            pl.BlockSpec((n1, n1), const),
            pl.BlockSpec((n1, n1), const),
            pl.BlockSpec((T2_GROUP, n1, LANE), lambda g, j: (j, 0, 0)),
            pl.BlockSpec((T2_GROUP, n1, LANE), lambda g, j: (j, 0, 0)),
        ],
        out_specs=[pl.BlockSpec((T2_GROUP, n1, FN_GROUP_DIM), lambda g, j: (j, 0, g))] * 2,
        out_shape=[jax.ShapeDtypeStruct((n2p, n1, FN_WIDTH), BF16)] * 2,
        scratch_shapes=[pltpu.VMEM((FN_GROUP_DIM // LANE, t + T2_GROUP, LANE), F32)],
        compiler_params=pltpu.CompilerParams(
            dimension_semantics=("parallel", "arbitrary"), vmem_limit_bytes=VMEM_LIMIT),
        name="dft_time_1",
    )(p, p_meta, cs, jnp.asarray(c1, BF16), jnp.asarray(s1, BF16), twr, twi)

    c2 = np.zeros((n2p, n2p))
    s2 = np.zeros((n2p, n2p))
    k2 = np.arange(n2, dtype=np.int64)
    c2[:n2, :n2], s2[:n2, :n2] = _phase_tables(k2[:, None] * k2[None, :], n2)
    tn = 512
    out = pl.pallas_call(
        functools.partial(_dft2_kernel, n2=n2, n2p=n2p, scale=scale),
        grid=(n1 // 16, FN_WIDTH // tn),
        in_specs=[
            pl.BlockSpec((n2p, 16, tn), lambda i, h: (0, i, h)),
            pl.BlockSpec((n2p, 16, tn), lambda i, h: (0, i, h)),
            pl.BlockSpec((n2p, n2p), lambda i, h: (0, 0)),
            pl.BlockSpec((n2p, n2p), lambda i, h: (0, 0)),
        ],
        out_specs=pl.BlockSpec((n2, 16, tn), lambda i, h: (0, i, h)),
        out_shape=jax.ShapeDtypeStruct((n2, n1, FN_WIDTH), BF16),
        scratch_shapes=[pltpu.VMEM((tn // LANE, n2p * 16, LANE), F32)] * 3,
        compiler_params=pltpu.CompilerParams(
            dimension_semantics=("parallel", "parallel"), vmem_limit_bytes=VMEM_LIMIT),
        name="dft_time_2",
    )(yre, yim, jnp.asarray(c2, BF16), jnp.asarray(s2, BF16))
    return out.reshape(t, FN_WIDTH)


def _merge_kernel(x_ref, hb_ref, st_ref, of_ref, ob_ref, za_ref, zf_ref, fr_ref, lig_ref, lib_ref, nw_ref,
                  wg_ref, bg_ref, wa_ref, wf_ref, wo_ref, lg_ref, lb_ref,
                  y_ref, ya_ref, yf_ref, m_ref, *, tn):
    dot = functools.partial(jnp.dot, preferred_element_type=F32)
    nw = nw_ref[...]
    for h in range(HEADS):
        hs = slice(h * HEAD_DIM, (h + 1) * HEAD_DIM)
        o = of_ref[:, hs].astype(F32) + ob_ref[:, hs].astype(F32)
        o = o * lax.rsqrt(jnp.mean(o * o, axis=-1, keepdims=True) + RMS_EPS) * nw
        ya_ref[:, hs] = (o * _silu(za_ref[:, hs].astype(F32))).astype(BF16)
    yf_ref[...] = (fr_ref[...].astype(F32) * _silu(zf_ref[...].astype(F32))).astype(BF16)

    hb = hb_ref[...]
    for j in range(D_MODEL // tn):
        ca = slice(j * tn, (j + 1) * tn)
        cf = slice(D_MODEL + j * tn, D_MODEL + (j + 1) * tn)
        ga = jax.nn.sigmoid(dot(hb, wg_ref[:, ca]) + bg_ref[:, ca])
        gf = jax.nn.sigmoid(dot(hb, wg_ref[:, cf]) + bg_ref[:, cf])
        m_ref[:, ca] = (ga * dot(ya_ref[...], wa_ref[:, ca]) + gf * dot(yf_ref[...], wf_ref[:, ca])).astype(BF16)
    out = dot(m_ref[...], wo_ref[...])

    st = st_ref[...]
    h = (x_ref[...] - st[:, BA_MU:BA_MU + 1]) * st[:, BA_RSTD:BA_RSTD + 1] * lig_ref[...] + lib_ref[...]
    y_ref[...] = _layer_norm(DEEPNORM_ALPHA * h + out, lg_ref[...], lb_ref[...])


def _merge(x, hb, ba, o_f, o_b, p, fr, ln_in_g, ln_in_b, nw, w_g, b_g, w_a, w_f, w_o, ln_g, ln_b, *,
           tm=256, tn=512):
    rows = x.shape[0]
    assert rows % tm == 0 and D_MODEL % tn == 0
    row_spec = lambda w, col=0: pl.BlockSpec((tm, w), lambda i: (i, col))
    return pl.pallas_call(
        functools.partial(_merge_kernel, tn=tn),
        grid=(rows // tm,),
        in_specs=[
            row_spec(D_MODEL), row_spec(D_MODEL), row_spec(LANE),
            row_spec(DN_WIDTH), row_spec(DN_WIDTH),
            row_spec(DN_WIDTH, PCOL_ZA), row_spec(FN_WIDTH, PCOL_ZF),
            row_spec(FN_WIDTH),
            _resident((1, D_MODEL)), _resident((1, D_MODEL)), _resident((1, HEAD_DIM)),
            _resident((D_MODEL, 2 * D_MODEL)), _resident((1, 2 * D_MODEL)),
            _resident((DN_WIDTH, D_MODEL)), _resident((FN_WIDTH, D_MODEL)), _resident((D_MODEL, D_MODEL)),
            _resident((1, D_MODEL)), _resident((1, D_MODEL)),
        ],
        out_specs=pl.BlockSpec((tm, D_MODEL), lambda i: (i, 0)),
        out_shape=jax.ShapeDtypeStruct((rows, D_MODEL), F32),
        scratch_shapes=[
            pltpu.VMEM((tm, DN_WIDTH), BF16),
            pltpu.VMEM((tm, FN_WIDTH), BF16),
            pltpu.VMEM((tm, D_MODEL), BF16),
        ],
        compiler_params=pltpu.CompilerParams(
            dimension_semantics=("parallel",), vmem_limit_bytes=VMEM_LIMIT_RESIDENT),
        name="merge",
    )(x, hb, ba, o_f, o_b, p, p, fr, ln_in_g, ln_in_b, nw, w_g, b_g, w_a, w_f, w_o, ln_g, ln_b)


def _encode(x, p_meta, ba_meta, ab_meta, wts, *, fft_factor=None):
    batch, seq, _ = x.shape
    rows = batch * seq
    t_full = N_META + seq
    x2 = x.reshape(rows, D_MODEL)

    p, ba, hb = _ln_inproj(x2, wts["ln_in_g"], wts["ln_in_b"], wts["w_p"], wts["w_ba"],
                           tm=_row_tile(rows, INPROJ_ROWS))

    tb = DELTA_BLOCK
    assert seq % tb == 0 and seq >= META_BLOCK
    nt = seq // tb
    p4 = p.reshape(batch, nt, tb, P_WIDTH)
    tail = p4[:, :, tb - 2:, :3 * DN_WIDTH]
    head = p4[:, :, :2, :3 * DN_WIDTH]
    meta_tail = jnp.broadcast_to(p_meta[None, None, N_META - 2:, :3 * DN_WIDTH], (batch, 1, 2, 3 * DN_WIDTH))
    prev = jnp.concatenate([meta_tail, tail[:, :-1]], axis=1)
    nxt = jnp.concatenate([head[:, 1:], jnp.zeros((batch, 1, 2, 3 * DN_WIDTH), BF16)], axis=1)
    pad6 = jnp.zeros((batch, nt, 6, 3 * DN_WIDTH), BF16)
    halo = jnp.concatenate([pad6, prev, nxt, pad6], axis=2)
    qkv, kt, bgc, gt = _prep(p, halo, ba, wts["conv_w8"], wts["gpar"], batch=batch, nt=nt, tm=tb)

    mb = META_BLOCK
    p_m = jnp.concatenate([
        jnp.broadcast_to(p_meta[None, :, :3 * DN_WIDTH], (batch, N_META, 3 * DN_WIDTH)),
        p.reshape(batch, seq, P_WIDTH)[:, :mb - N_META, :3 * DN_WIDTH]], axis=1).reshape(batch * mb, 3 * DN_WIDTH)
    ba_m = jnp.concatenate([
        jnp.broadcast_to(ba_meta[None], (batch, N_META, LANE)),
        ba.reshape(batch, seq, LANE)[:, :mb - N_META]], axis=1).reshape(batch * mb, LANE)
    halo_m = jnp.zeros((batch, 1, 16, 3 * DN_WIDTH), BF16)
    qkv_m, kt_m, bgc_m, gt_m = _prep(p_m, halo_m, ba_m, wts["conv_w8"], wts["gpar"],
                                     batch=batch, nt=1, tm=mb, valid_rows=N_META)
    s_zero = jnp.zeros((batch, NPAIR, HEAD_DIM, PAIR), F32)
    _, _, s_meta = _delta(qkv_m, kt_m, bgc_m, gt_m, s_zero, batch=batch, nblk=1, tb=mb)

    o_f, o_b, _ = _delta(qkv, kt, bgc, gt, s_meta, batch=batch, nblk=nt, tb=tb)

    n1, n2 = fft_factor if fft_factor is not None else _fft_factor(t_full)
    if n2 == 1:
        a_x, b_x = _caxis(p, wts["cs"])
        fr = _dft_dense(a_x, b_x, ab_meta[0], ab_meta[1], batch=batch, seq=seq)
    else:
        assert batch == 1
        fr = _dft_four_step(p, p_meta, wts["cs"], seq=seq, factor=(n1, n2))

    y = _merge(x2, hb, ba, o_f, o_b, p, fr, wts["ln_in_g"], wts["ln_in_b"], wts["nw"], wts["w_g"], wts["b_g"],
               wts["w_a"], wts["w_f"], wts["w_o"], wts["ln_g"], wts["ln_b"], tm=_row_tile(rows, MERGE_ROWS))
    return y.reshape(batch, seq, D_MODEL)


def _prepare_weights(ln_in_g, ln_in_b, w_in, b_gate, conv_w, a_log, dt_bias, dn_norm_w,
                     w_proj_a, w_proj_f, w_out, ln_g, ln_b):
    w = w_in[0]
    col_b = 4 * DN_WIDTH
    col_f = col_b + 4 * HEADS
    col_g = col_f + 2 * FN_WIDTH
    w_p = jnp.concatenate([w[:, :col_b], w[:, col_f:col_g]], axis=1).astype(BF16)
    w_ba = jnp.pad(w[:, col_b:col_f], ((0, 0), (0, LANE - 4 * HEADS))).astype(BF16)
    gpar = jnp.zeros((8, LANE), F32)
    gpar = gpar.at[0, BG_G:BG_GC].set(a_log[0].reshape(-1).astype(F32))
    gpar = gpar.at[1, BG_G:BG_GC].set(dt_bias[0].reshape(-1).astype(F32))
    k = np.arange(FN_GROUP_DIM, dtype=np.int64)
    ang = 2.0 * np.pi * ((k[:, None] * k[None, :]) % FN_GROUP_DIM) / FN_GROUP_DIM
    cs = jnp.asarray(np.concatenate([np.cos(ang), np.sin(ang)], axis=1), BF16)
    return {
        "ln_in_g": ln_in_g.reshape(1, D_MODEL).astype(F32),
        "ln_in_b": ln_in_b.reshape(1, D_MODEL).astype(F32),
        "w_p": w_p,
        "w_ba": w_ba,
        "conv_w8": jnp.pad(conv_w[0].astype(F32), ((0, 8 - CONV_K), (0, 0))),
        "gpar": gpar,
        "cs": cs,
        "nw": dn_norm_w[0].reshape(1, HEAD_DIM).astype(F32),
        "w_g": w[:, col_g:].astype(BF16),
        "b_g": b_gate[0].reshape(1, 2 * D_MODEL).astype(F32),
        "w_a": w_proj_a[0].astype(BF16),
        "w_f": w_proj_f[0].astype(BF16),
        "w_o": w_out[0].astype(BF16),
        "ln_g": ln_g[0].reshape(1, D_MODEL).astype(F32),
        "ln_b": ln_b[0].reshape(1, D_MODEL).astype(F32),
    }


def kernel(x_prompt, x_sample, meta_tokens, ln_in_g, ln_in_b, w_in, b_gate, conv_w, a_log, dt_bias, dn_norm_w,
           w_proj_a, w_proj_f, w_out, ln_g, ln_b):
    wts = _prepare_weights(ln_in_g, ln_in_b, w_in, b_gate, conv_w, a_log, dt_bias, dn_norm_w,
                           w_proj_a, w_proj_f, w_out, ln_g, ln_b)
    p_meta, ba_meta, _ = _ln_inproj(meta_tokens.astype(F32), wts["ln_in_g"], wts["ln_in_b"],
                                    wts["w_p"], wts["w_ba"], tm=N_META)
    ab_meta = _caxis(p_meta, wts["cs"])
    y_prompt = _encode(x_prompt, p_meta, ba_meta, ab_meta, wts)
    y_sample = _encode(x_sample, p_meta, ba_meta, ab_meta, wts)
    return (y_prompt, y_sample)
```

```python
import functools
import math

import numpy as np
import jax
import jax.numpy as jnp
from jax import lax
from jax.experimental import pallas as pl
from jax.experimental.pallas import tpu as pltpu

F32 = jnp.float32
BF16 = jnp.bfloat16

D_MODEL = 2048
N_META = 16
HEADS = 8
HEAD_DIM = 128
DN_WIDTH = HEADS * HEAD_DIM
CONV_K = 5
FN_GROUPS = 4
FN_GROUP_DIM = 256
FN_WIDTH = FN_GROUPS * FN_GROUP_DIM
DEPTH = 1
DEEPNORM_ALPHA = (2 * DEPTH) ** 0.25
LN_EPS = 1e-5
RMS_EPS = 1e-6
L2_EPS = 1e-6

LANE = 128
CHUNK = HEAD_DIM
PAIR = 2 * HEAD_DIM
NPAIR = HEADS // 2
INV_BASE = 16
PHASE_A_GROUP = 16
DELTA_BLOCK = 256
META_BLOCK = CHUNK
T2_GROUP = 16
VMEM_LIMIT = 56 * 1024 * 1024
VMEM_LIMIT_RESIDENT = 60 * 1024 * 1024
INPROJ_ROWS = 512
MERGE_ROWS = 256

PCOL_Q, PCOL_K, PCOL_V, PCOL_ZA, PCOL_F, PCOL_ZF = range(6)
P_WIDTH = 6 * 1024
BG_BETA, BG_G, BG_GC, BG_TOT = 0, 16, 32, 48
BA_MU, BA_RSTD = 32, 33
GT_ROWS = 64


def _mm(a, b):
    return jnp.dot(a.astype(BF16), b.astype(BF16), preferred_element_type=F32)


def _layer_norm(x, g, b):
    mu = jnp.mean(x, axis=-1, keepdims=True)
    xc = x - mu
    var = jnp.mean(xc * xc, axis=-1, keepdims=True)
    return xc * lax.rsqrt(var + LN_EPS) * g + b


def _silu(x):
    return x * jax.nn.sigmoid(x)


def _row_tile(rows, cap, mult=16):
    best = None
    for t in range(mult, min(rows, cap) + 1, mult):
        if rows % t == 0:
            best = t
    assert best is not None, (rows, cap, mult)
    return best


def _ln_inproj_kernel(x_ref, g_ref, b_ref, w_ref, wba_ref, p_ref, ba_ref, hb_ref, *, tn):
    x = x_ref[...]
    mu = jnp.mean(x, axis=-1, keepdims=True)
    xc = x - mu
    rstd = lax.rsqrt(jnp.mean(xc * xc, axis=-1, keepdims=True) + LN_EPS)
    hb = (xc * rstd * g_ref[...] + b_ref[...]).astype(BF16)
    hb_ref[...] = hb
    lane = lax.broadcasted_iota(jnp.int32, ba_ref.shape, 1)
    ba = jnp.dot(hb, wba_ref[...], preferred_element_type=F32)
    ba_ref[...] = jnp.where(lane == BA_MU, mu, jnp.where(lane == BA_RSTD, rstd, ba))
    for j in range(P_WIDTH // tn):
        cs = slice(j * tn, (j + 1) * tn)
        p_ref[:, cs] = jnp.dot(hb, w_ref[:, cs], preferred_element_type=F32).astype(p_ref.dtype)


def _resident(shape):
    return pl.BlockSpec(shape, lambda i: (0,) * len(shape), pipeline_mode=pl.Buffered(1))


def _ln_inproj(x, ln_g, ln_b, w_p, w_ba, *, tm, tn=1024):
    rows = x.shape[0]
    assert rows % tm == 0 and P_WIDTH % tn == 0
    return pl.pallas_call(
        functools.partial(_ln_inproj_kernel, tn=tn),
        grid=(rows // tm,),
        in_specs=[
            pl.BlockSpec((tm, D_MODEL), lambda i: (i, 0)),
            _resident((1, D_MODEL)),
            _resident((1, D_MODEL)),
            _resident((D_MODEL, P_WIDTH)),
            _resident((D_MODEL, LANE)),
        ],
        out_specs=[
            pl.BlockSpec((tm, P_WIDTH), lambda i: (i, 0)),
            pl.BlockSpec((tm, LANE), lambda i: (i, 0)),
            pl.BlockSpec((tm, D_MODEL), lambda i: (i, 0)),
        ],
        out_shape=[
            jax.ShapeDtypeStruct((rows, P_WIDTH), BF16),
            jax.ShapeDtypeStruct((rows, LANE), F32),
            jax.ShapeDtypeStruct((rows, D_MODEL), BF16),
        ],
        compiler_params=pltpu.CompilerParams(
            dimension_semantics=("parallel",), vmem_limit_bytes=VMEM_LIMIT_RESIDENT),
        name="ln_inproj",
    )(x, ln_g, ln_b, w_p, w_ba)


def _split3(x):
    hi = x.astype(BF16)
    r1 = x - hi.astype(F32)
    mid = r1.astype(BF16)
    lo = (r1 - mid.astype(F32)).astype(BF16)
    return hi, mid, lo


def _prep_kernel(p_ref, halo_ref, ba_ref, cw_ref, gpar_ref, qkv_ref, kt_ref, bgc_ref, gt_ref, *, tm, valid_rows):
    cw = cw_ref[...]
    row = lax.broadcasted_iota(jnp.int32, (tm, 1), 0)
    if valid_rows is not None:
        keep = row < valid_rows

    shifts = (-2, -1, 1, 2)
    r = lax.broadcasted_iota(jnp.int32, (tm, tm), 0)
    c = lax.broadcasted_iota(jnp.int32, (tm, tm), 1)
    rh = lax.broadcasted_iota(jnp.int32, (8, 16), 0)
    ch = lax.broadcasted_iota(jnp.int32, (8, 16), 1)
    sel = jnp.concatenate([jnp.where(c == r + s, 1.0, 0.0) for s in shifts], axis=0).astype(BF16)
    sel_halo = jnp.concatenate(
        [jnp.where((rh < -s) & (ch == 8 + rh + s) if s < 0 else (rh >= 8 - s) & (ch == rh + s), 1.0, 0.0)
         for s in shifts], axis=0).astype(BF16)

    def shifted(j):
        return jnp.dot(sel, p_ref[:, j * PAIR:(j + 1) * PAIR], preferred_element_type=F32)

    nchunks = 3 * DN_WIDTH // PAIR
    ahead = shifted(0)
    for j in range(nchunks):
        ls = slice(j * PAIR, (j + 1) * PAIR)
        moved = ahead
        if j + 1 < nchunks:
            ahead = shifted(j + 1)
        acc = p_ref[:, ls].astype(F32) * cw[2:3, ls]
        for n, s in enumerate(shifts):
            acc = acc + moved[n * tm:(n + 1) * tm] * cw[2 + s:3 + s, ls]
        edge = jnp.dot(sel_halo, halo_ref[:, ls], preferred_element_type=F32)
        top = acc[0:8]
        bottom = acc[tm - 8:tm]
        for n, s in enumerate(shifts):
            term = edge[8 * n:8 * n + 8] * cw[2 + s:3 + s, ls]
            if s < 0:
                top = top + term
            else:
                bottom = bottom + term
        acc = jnp.concatenate([top, acc[8:tm - 8], bottom], axis=0)
        act = _silu(acc)
        if valid_rows is not None:
            act = jnp.where(keep, act, 0.0)
        if j * PAIR >= 2 * DN_WIDTH:
            qkv_ref[:, ls] = act.astype(qkv_ref.dtype)
            continue
        scale = HEAD_DIM ** -0.5 if j * PAIR < DN_WIDTH else 1.0
        for half in range(PAIR // HEAD_DIM):
            t = act[:, half * HEAD_DIM:(half + 1) * HEAD_DIM]
            tn = t * (lax.rsqrt(jnp.sum(t * t, axis=-1, keepdims=True) + L2_EPS) * scale)
            col = j * PAIR + half * HEAD_DIM
            qkv_ref[:, col:col + HEAD_DIM] = tn.astype(qkv_ref.dtype)
            if col >= DN_WIDTH:
                kt_ref[col - DN_WIDTH:col - DN_WIDTH + HEAD_DIM, :] = tn.T.astype(kt_ref.dtype)

    ba = ba_ref[...]
    gpar = gpar_ref[...]
    lane = lax.broadcasted_iota(jnp.int32, (tm, LANE), 1)
    beta = jax.nn.sigmoid(ba)
    z = ba + gpar[1:2, :]
    softplus = jnp.maximum(z, 0.0) + jnp.log1p(jnp.exp(-jnp.abs(z)))
    g = -jnp.exp(gpar[0:1, :]) * softplus
    is_g = (lane >= BG_G) & (lane < BG_GC)
    g = jnp.where(is_g, g, 0.0)
    beta = jnp.where(lane < BG_G, beta, 0.0)
    if valid_rows is not None:
        g = jnp.where(keep, g, 0.0)
        beta = jnp.where(keep, beta, 0.0)

    r = lax.broadcasted_iota(jnp.int32, (tm, tm), 0)
    c = lax.broadcasted_iota(jnp.int32, (tm, tm), 1)
    same = (r // CHUNK) == (c // CHUNK)
    m_pre = jnp.where(same & (r >= c), 1.0, 0.0).astype(BF16)
    m_suf = jnp.where(same & (r <= c), 1.0, 0.0).astype(BF16)
    pre = jnp.zeros((tm, LANE), F32)
    suf = jnp.zeros((tm, LANE), F32)
    for part in _split3(g):
        pre = pre + jnp.dot(m_pre, part, preferred_element_type=F32)
        suf = suf + jnp.dot(m_suf, part, preferred_element_type=F32)
    tot = pre + suf - g
    fwd_lane = lane < BG_G + HEADS
    gc = jnp.where(fwd_lane, pre, suf)
    out = beta + g + pltpu.roll(gc, BG_GC - BG_G, axis=1) + pltpu.roll(tot, BG_TOT - BG_G, axis=1)
    bgc_ref[...] = out
    gt_ref[...] = out.T[:GT_ROWS, :]


def _prep(p, halo, ba, conv_w8, gpar, *, batch, nt, tm, valid_rows=None):
    rows = batch * nt * tm
    rmap = lambda b, i: (b * nt + i, 0)
    cmap = lambda b, i: (0, b * nt + i)
    return pl.pallas_call(
        functools.partial(_prep_kernel, tm=tm, valid_rows=valid_rows),
        grid=(batch, nt),
        in_specs=[
            pl.BlockSpec((tm, 3 * DN_WIDTH), rmap),
            pl.BlockSpec((None, None, 16, 3 * DN_WIDTH), lambda b, i: (b, i, 0, 0)),
            pl.BlockSpec((tm, LANE), rmap),
            pl.BlockSpec((8, 3 * DN_WIDTH), lambda b, i: (0, 0)),
            pl.BlockSpec((8, LANE), lambda b, i: (0, 0)),
        ],
        out_specs=[
            pl.BlockSpec((tm, 3 * DN_WIDTH), rmap),
            pl.BlockSpec((DN_WIDTH, tm), cmap),
            pl.BlockSpec((tm, LANE), rmap),
            pl.BlockSpec((GT_ROWS, tm), cmap),
        ],
        out_shape=[
            jax.ShapeDtypeStruct((rows, 3 * DN_WIDTH), BF16),
            jax.ShapeDtypeStruct((DN_WIDTH, rows), BF16),
            jax.ShapeDtypeStruct((rows, LANE), F32),
            jax.ShapeDtypeStruct((GT_ROWS, rows), F32),
        ],
        compiler_params=pltpu.CompilerParams(
            dimension_semantics=("parallel", "parallel"), vmem_limit_bytes=VMEM_LIMIT),
        name="prep",
    )(p, halo, ba, conv_w8, gpar)


def _bd(y):
    yb = y.astype(BF16)
    z = jnp.zeros((HEAD_DIM, HEAD_DIM), BF16)
    top = jnp.concatenate([yb[:, :HEAD_DIM], z], axis=1)
    bot = jnp.concatenate([z, yb[:, HEAD_DIM:]], axis=1)
    return jnp.concatenate([top, bot], axis=0)


def _overlay(m, b):
    out = m[0:b]
    for g in range(1, CHUNK // b):
        out = out + m[g * b:(g + 1) * b]
    return out


def _spread(m_ov, same):
    return jnp.where(same, jnp.concatenate([m_ov] * (CHUNK // m_ov.shape[0]), axis=0), 0.0)


def _pair_cols(bg, lane0):
    a = jnp.broadcast_to(bg[:, lane0:lane0 + 1], (CHUNK, HEAD_DIM))
    b = jnp.broadcast_to(bg[:, lane0 + 1:lane0 + 2], (CHUNK, HEAD_DIM))
    return jnp.concatenate([a, b], axis=1)


def _pair_rows(gt, row0, rs):
    return jnp.concatenate([gt[row0:row0 + 1, rs], gt[row0 + 1:row0 + 2, rs]], axis=1)


def _delta_kernel(qf_ref, kf_ref, vf_ref, ktf_ref, bgf_ref, gtf_ref,
                  qb_ref, kb_ref, vb_ref, ktb_ref, bgb_ref, gtb_ref, s0_ref,
                  of_ref, ob_ref, sfin_ref,
                  s_scr, pk_scr, kq_scr, bv_scr, gl_scr, *, nchunk):
    i = pl.program_id(1)

    @pl.when(i == 0)
    def _():
        s_scr[0] = s0_ref[...]
        s_scr[1] = jnp.zeros_like(s_scr[1])

    row = lax.broadcasted_iota(jnp.int32, (CHUNK, PAIR), 0)
    col = lax.broadcasted_iota(jnp.int32, (CHUNK, PAIR), 1) % HEAD_DIM
    eye2 = jnp.where(row == col, 1.0, 0.0).astype(F32)
    dirs = (
        dict(q=qf_ref, k=kf_ref, v=vf_ref, kt=ktf_ref, bg=bgf_ref, gt=gtf_ref, o=of_ref, dcol=0,
             incl=row >= col, strict=row > col, order=tuple(range(nchunk))),
        dict(q=qb_ref, k=kb_ref, v=vb_ref, kt=ktb_ref, bg=bgb_ref, gt=gtb_ref, o=ob_ref, dcol=HEADS,
             incl=row <= col, strict=row < col, order=tuple(range(nchunk - 1, -1, -1))),
    )
    units = [(d, step, p) for d in range(2) for step in range(nchunk) for p in range(NPAIR)]
    steps = int(math.log2(INV_BASE)) - 1
    same_block = {}
    size = INV_BASE
    while size < CHUNK:
        same_block[size] = (row // size) == (col // size)
        size *= 2
    eye_base = _overlay(eye2, INV_BASE)

    def phase_a(group):
        st = []
        for u, (d, step, p) in group:
            dr = dirs[d]
            c = dr["order"][step]
            rs = slice(c * CHUNK, (c + 1) * CHUNK)
            ls = slice(p * PAIR, (p + 1) * PAIR)
            j0 = dr["dcol"] + 2 * p
            bg = dr["bg"][rs, :]
            gt = dr["gt"]
            beta = _pair_cols(bg, BG_BETA + j0)
            gc = _pair_cols(bg, BG_GC + j0)
            tot = _pair_cols(bg, BG_TOT + j0)
            gr = _pair_rows(gt, BG_GC + j0, rs)
            totr = _pair_rows(gt, BG_TOT + j0, rs)
            decay = jnp.exp(jnp.where(dr["incl"], gc - gr, -1e30))
            kt2 = jnp.concatenate([dr["kt"][p * PAIR:p * PAIR + HEAD_DIM, rs],
                                   dr["kt"][p * PAIR + HEAD_DIM:(p + 1) * PAIR, rs]], axis=1)
            qb = dr["q"][rs, ls]
            kb = dr["k"][rs, ls]
            qk = jnp.dot(jnp.concatenate([qb, kb], axis=0), _bd(kt2), preferred_element_type=F32)
            lm = jnp.where(dr["strict"], beta * qk[CHUNK:] * decay, 0.0)
            egc = jnp.exp(gc)
            kq_scr[u] = jnp.concatenate([(beta * egc) * kb.astype(F32), egc * qb.astype(F32)],
                                        axis=0).astype(BF16)
            bv_scr[u] = beta * dr["v"][rs, ls].astype(F32)
            gl_scr[u] = jnp.exp(tot[0:8, :])
            st.append(dict(lm=lm, pm=qk[:CHUNK] * decay, kdt=kt2.astype(F32) * jnp.exp(totr - gr)))
        b0 = INV_BASE
        for s in st:
            x_full = -jnp.where(same_block[b0], s["lm"], 0.0)
            s["xw"] = _bd(x_full)
            s["x"] = _overlay(x_full, b0)
            s["t"] = eye_base + s["x"]
        for s in st:
            s["p"] = _mm(s["x"], s["xw"])
        for it in range(steps):
            last = it + 1 == steps
            for s in st:
                lhs = s["t"] if last else jnp.concatenate([s["t"], s["p"]], axis=0)
                both = _mm(lhs, _bd(_spread(s["p"], same_block[b0])))
                s["t"] = s["t"] + both[:b0]
                if not last:
                    s["p"] = both[b0:]
        size = b0
        while size < CHUNK:
            top = 2 * size == CHUNK
            join = ~same_block[size] if top else same_block[2 * size] & ~same_block[size]
            for s in st:
                t_full = _spread(s["t"], same_block[size])
                n_full = jnp.where(join, s["lm"], 0.0)
                s["t2"] = t_full if top else _overlay(t_full, 2 * size)
                s["a"] = _mm(n_full if top else _overlay(n_full, 2 * size), _bd(t_full))
            for s in st:
                a_full = s["a"] if top else _spread(s["a"], same_block[2 * size])
                s["t"] = s["t2"] - _mm(s["t2"], _bd(a_full))
            size *= 2
        for (u, _), s in zip(group, st):
            pk_scr[u] = _mm(jnp.concatenate([s["pm"], s["kdt"]], axis=0), _bd(s["t"])).astype(BF16)

    indexed = list(enumerate(units))
    for g0 in range(0, len(indexed), PHASE_A_GROUP):
        phase_a(indexed[g0:g0 + PHASE_A_GROUP])

    for step in range(nchunk):
        active = [(u, un) for u, un in indexed if un[1] == step]
        xs = [jnp.dot(kq_scr[u], _bd(s_scr[d, p]), preferred_element_type=F32) for u, (d, _, p) in active]
        rr = [bv_scr[u] - x[:CHUNK] for (u, _), x in zip(active, xs)]
        for (u, (d, _, p)), x, r in zip(active, xs, rr):
            dr = dirs[d]
            c = dr["order"][step]
            res = jnp.dot(pk_scr[u], _bd(r), preferred_element_type=F32)
            dr["o"][c * CHUNK:(c + 1) * CHUNK, p * PAIR:(p + 1) * PAIR] = (
                x[CHUNK:] + res[:CHUNK]).astype(dr["o"].dtype)
            s_scr[d, p] = gl_scr[u][0:1, :] * s_scr[d, p] + res[CHUNK:]

    @pl.when(i == pl.num_programs(1) - 1)
    def _():
        sfin_ref[...] = s_scr[0]


def _delta(qkv, kt, bgc, gt, s0, *, batch, nblk, tb):
    rows = batch * nblk * tb
    assert tb % CHUNK == 0 and CHUNK == HEAD_DIM
    nchunk = tb // CHUNK
    nunit = 2 * nchunk * NPAIR
    fwd = lambda b, i: b * nblk + i
    bwd = lambda b, i: b * nblk + (nblk - 1 - i)

    def dir_specs(blk):
        return [
            pl.BlockSpec((tb, DN_WIDTH), lambda b, i: (blk(b, i), 0)),
            pl.BlockSpec((tb, DN_WIDTH), lambda b, i: (blk(b, i), 1)),
            pl.BlockSpec((tb, DN_WIDTH), lambda b, i: (blk(b, i), 2)),
            pl.BlockSpec((DN_WIDTH, tb), lambda b, i: (0, blk(b, i))),
            pl.BlockSpec((tb, LANE), lambda b, i: (blk(b, i), 0)),
            pl.BlockSpec((GT_ROWS, tb), lambda b, i: (0, blk(b, i))),
        ]

    sspec = pl.BlockSpec((None, NPAIR, HEAD_DIM, PAIR), lambda b, i: (b, 0, 0, 0))
    return pl.pallas_call(
        functools.partial(_delta_kernel, nchunk=nchunk),
        grid=(batch, nblk),
        in_specs=dir_specs(fwd) + dir_specs(bwd) + [sspec],
        out_specs=[
            pl.BlockSpec((tb, DN_WIDTH), lambda b, i: (fwd(b, i), 0)),
            pl.BlockSpec((tb, DN_WIDTH), lambda b, i: (bwd(b, i), 0)),
            sspec,
        ],
        out_shape=[
            jax.ShapeDtypeStruct((rows, DN_WIDTH), BF16),
            jax.ShapeDtypeStruct((rows, DN_WIDTH), BF16),
            jax.ShapeDtypeStruct((batch, NPAIR, HEAD_DIM, PAIR), F32),
        ],
        scratch_shapes=[
            pltpu.VMEM((2, NPAIR, HEAD_DIM, PAIR), F32),
            pltpu.VMEM((nunit, CHUNK + HEAD_DIM, PAIR), BF16),
            pltpu.VMEM((nunit, 2 * CHUNK, PAIR), BF16),
            pltpu.VMEM((nunit, CHUNK, PAIR), F32),
            pltpu.VMEM((nunit, 8, PAIR), F32),
        ],
        compiler_params=pltpu.CompilerParams(
            dimension_semantics=("parallel", "arbitrary"), vmem_limit_bytes=VMEM_LIMIT),
        name="delta",
    )(qkv, qkv, qkv, kt, bgc, gt, qkv, qkv, qkv, kt, bgc, gt, s0)


def _phase_tables(num, den):
    ang = 2.0 * np.pi * (num % den).astype(np.float64) / den
    return np.cos(ang), np.sin(ang)


def _fft_factor(t):
    if t <= 4096:
        return t, 1
    best = None
    for n1 in range(16, t, 16):
        if t % n1:
            continue
        n2 = t // n1
        n2p = -(-n2 // T2_GROUP) * T2_GROUP
        cost = 4 * n1 + 2 * n2p
        if best is None or cost < best[0]:
            best = (cost, n1, n2)
    assert best is not None, t
    return best[1], best[2]


def _caxis_kernel(f_ref, cs_ref, a_ref, b_ref):
    cs = cs_ref[...]
    for g in range(FN_GROUPS):
        sl = slice(g * FN_GROUP_DIM, (g + 1) * FN_GROUP_DIM)
        ab = jnp.dot(f_ref[:, sl], cs, preferred_element_type=F32)
        a_ref[:, sl] = ab[:, :FN_GROUP_DIM].astype(a_ref.dtype)
        b_ref[:, sl] = ab[:, FN_GROUP_DIM:].astype(b_ref.dtype)


def _caxis(p, cs):
    rows = p.shape[0]
    tr = _row_tile(rows, 1024)
    spec = pl.BlockSpec((tr, FN_WIDTH), lambda i: (i, 0))
    return pl.pallas_call(
        _caxis_kernel,
        grid=(rows // tr,),
        in_specs=[pl.BlockSpec((tr, FN_WIDTH), lambda i: (i, PCOL_F)),
                  pl.BlockSpec((FN_GROUP_DIM, 2 * FN_GROUP_DIM), lambda i: (0, 0))],
        out_specs=[spec, spec],
        out_shape=[jax.ShapeDtypeStruct((rows, FN_WIDTH), BF16)] * 2,
        compiler_params=pltpu.CompilerParams(
            dimension_semantics=("parallel",), vmem_limit_bytes=VMEM_LIMIT),
        name="dft_channel",
    )(p, cs)


def _dft_dense_kernel(ax_ref, bx_ref, am_ref, bm_ref, cx_ref, sx_ref, cm_ref, sm_ref, o_ref, *, scale):
    dot = functools.partial(jnp.dot, preferred_element_type=F32)
    y = (dot(cx_ref[...], ax_ref[...]) - dot(sx_ref[...], bx_ref[...])
         + dot(cm_ref[...], am_ref[...]) - dot(sm_ref[...], bm_ref[...]))
    o_ref[...] = (y * scale).astype(o_ref.dtype)


def _dft_dense(a_x, b_x, a_m, b_m, *, batch, seq):
    t = N_META + seq
    scale = 1.0 / math.sqrt(t * FN_GROUP_DIM)
    k = np.arange(seq, dtype=np.int64)[:, None] + N_META
    cx, sx = _phase_tables(k * (np.arange(seq, dtype=np.int64)[None, :] + N_META), t)
    cm, sm = _phase_tables(k * np.arange(N_META, dtype=np.int64)[None, :], t)
    tmi = _row_tile(seq, 512)
    nm = seq // tmi
    xspec = pl.BlockSpec((seq, FN_WIDTH), lambda bb, m: (bb, 0))
    mspec = pl.BlockSpec((N_META, FN_WIDTH), lambda bb, m: (0, 0))
    return pl.pallas_call(
        functools.partial(_dft_dense_kernel, scale=scale),
        grid=(batch, nm),
        in_specs=[xspec, xspec, mspec, mspec,
                  pl.BlockSpec((tmi, seq), lambda bb, m: (m, 0)),
                  pl.BlockSpec((tmi, seq), lambda bb, m: (m, 0)),
                  pl.BlockSpec((tmi, N_META), lambda bb, m: (m, 0)),
                  pl.BlockSpec((tmi, N_META), lambda bb, m: (m, 0))],
        out_specs=pl.BlockSpec((tmi, FN_WIDTH), lambda bb, m: (bb * nm + m, 0)),
        out_shape=jax.ShapeDtypeStruct((batch * seq, FN_WIDTH), BF16),
        compiler_params=pltpu.CompilerParams(
            dimension_semantics=("parallel", "parallel"), vmem_limit_bytes=VMEM_LIMIT),
        name="dft_time_dense",
    )(a_x, b_x, a_m, b_m, jnp.asarray(cx, BF16), jnp.asarray(sx, BF16), jnp.asarray(cm, BF16), jnp.asarray(sm, BF16))


def _dft1_kernel(px_ref, pm_ref, cs_ref, c_ref, s_ref, twr_ref, twi_ref, yre_ref, yim_ref, scr, *, t, n1, n2):
    j = pl.program_id(1)
    halves = FN_GROUP_DIM // LANE

    @pl.when(j == 0)
    def _():
        for c in range(halves):
            ls = slice(c * LANE, (c + 1) * LANE)
            scr[c, 0:N_META, :] = pm_ref[:, ls].astype(F32)
            scr[c, N_META:t, :] = px_ref[:, ls].astype(F32)
            scr[c, t:t + T2_GROUP, :] = jnp.zeros((T2_GROUP, LANE), F32)

    dot = functools.partial(jnp.dot, preferred_element_type=F32)
    cs = cs_ref[...]
    c1 = c_ref[...]
    s1 = s_ref[...]
    xs = [jnp.concatenate([scr[c, pl.ds(j * T2_GROUP + tt, n1, stride=n2), :] for c in range(halves)],
                          axis=1).astype(BF16) for tt in range(T2_GROUP)]
    abs_ = [dot(x, cs) for x in xs]
    aa = [ab[:, :FN_GROUP_DIM].astype(BF16) for ab in abs_]
    bb = [ab[:, FN_GROUP_DIM:].astype(BF16) for ab in abs_]
    yre = [dot(c1, a) - dot(s1, b) for a, b in zip(aa, bb)]
    yim = [-(dot(c1, b) + dot(s1, a)) for a, b in zip(aa, bb)]
    for tt in range(T2_GROUP):
        live = j * T2_GROUP + tt < n2
        wr = twr_ref[tt]
        wi = twi_ref[tt]
        for c in range(halves):
            ls = slice(c * LANE, (c + 1) * LANE)
            re = yre[tt][:, ls]
            im = yim[tt][:, ls]
            yre_ref[tt, :, ls] = jnp.where(live, re * wr - im * wi, 0.0).astype(yre_ref.dtype)
            yim_ref[tt, :, ls] = jnp.where(live, re * wi + im * wr, 0.0).astype(yim_ref.dtype)


def _dft2_kernel(yre_ref, yim_ref, c_ref, s_ref, o_ref, re_scr, im_scr, o_scr, *, n2, n2p, scale):
    ncol = o_ref.shape[-1] // LANE

    def fill(i2, carry):
        rows = pl.ds(pl.multiple_of(i2 * 16, 16), 16)
        for c in range(ncol):
            ls = slice(c * LANE, (c + 1) * LANE)
            re_scr[c, rows, :] = yre_ref[i2, :, ls].astype(F32)
            im_scr[c, rows, :] = yim_ref[i2, :, ls].astype(F32)
        return carry

    lax.fori_loop(0, n2p, fill, 0)
    dot = functools.partial(jnp.dot, preferred_element_type=F32)
    c2 = c_ref[...]
    s2 = s_ref[...]
    for k in range(16):
        gather = lambda scr: jnp.concatenate(
            [scr[c, pl.ds(k, n2p, stride=16), :] for c in range(ncol)], axis=1).astype(BF16)
        y = (dot(c2, gather(re_scr)) + dot(s2, gather(im_scr))) * scale
        for c in range(ncol):
            o_scr[c, pl.ds(k, n2p, stride=16), :] = y[:, c * LANE:(c + 1) * LANE]

    def drain(i2, carry):
        rows = pl.ds(pl.multiple_of(i2 * 16, 16), 16)
        for c in range(ncol):
            o_ref[i2, :, c * LANE:(c + 1) * LANE] = o_scr[c, rows, :].astype(o_ref.dtype)
        return carry

    lax.fori_loop(0, n2, drain, 0)


def _dft_four_step(p, p_meta, cs, *, seq, factor=None):
    t = N_META + seq
    n1, n2 = factor if factor is not None else _fft_factor(t)
    assert n1 * n2 == t and n1 % 16 == 0 and n2 > 1
    n2p = -(-n2 // T2_GROUP) * T2_GROUP
    scale = 1.0 / math.sqrt(t * FN_GROUP_DIM)
    k1 = np.arange(n1, dtype=np.int64) + N_META
    c1, s1 = _phase_tables(k1[:, None] * np.arange(n1, dtype=np.int64)[None, :], n1)
    twc, tws = _phase_tables(np.arange(n2p, dtype=np.int64)[:, None] * k1[None, :], t)
    twr = jnp.asarray(np.broadcast_to(twc[:, :, None], (n2p, n1, LANE)), F32)
    twi = jnp.asarray(np.broadcast_to(-tws[:, :, None], (n2p, n1, LANE)), F32)
    fcol = lambda g, j: (0, PCOL_F * FN_GROUPS + g)
    const = lambda g, j: (0, 0)
    yre, yim = pl.pallas_call(
        functools.partial(_dft1_kernel, t=t, n1=n1, n2=n2),
        grid=(FN_GROUPS, n2p // T2_GROUP),
        in_specs=[
            pl.BlockSpec((seq, FN_GROUP_DIM), fcol),
            pl.BlockSpec((N_META, FN_GROUP_DIM), fcol),
            pl.BlockSpec((FN_GROUP_DIM, 2 * FN_GROUP_DIM), const),
            pl.BlockSpec((n1, n1), const),
            pl.BlockSpec((n1, n1), const),
            pl.BlockSpec((T2_GROUP, n1, LANE), lambda g, j: (j, 0, 0)),
            pl.BlockSpec((T2_GROUP, n1, LANE), lambda g, j: (j, 0, 0)),
        ],
        out_specs=[pl.BlockSpec((T2_GROUP, n1, FN_GROUP_DIM), lambda g, j: (j, 0, g))] * 2,
        out_shape=[jax.ShapeDtypeStruct((n2p, n1, FN_WIDTH), BF16)] * 2,
        scratch_shapes=[pltpu.VMEM((FN_GROUP_DIM // LANE, t + T2_GROUP, LANE), F32)],
        compiler_params=pltpu.CompilerParams(
            dimension_semantics=("parallel", "arbitrary"), vmem_limit_bytes=VMEM_LIMIT),
        name="dft_time_1",
    )(p, p_meta, cs, jnp.asarray(c1, BF16), jnp.asarray(s1, BF16), twr, twi)

    c2 = np.zeros((n2p, n2p))
    s2 = np.zeros((n2p, n2p))
    k2 = np.arange(n2, dtype=np.int64)
    c2[:n2, :n2], s2[:n2, :n2] = _phase_tables(k2[:, None] * k2[None, :], n2)
    tn = 512
    out = pl.pallas_call(
        functools.partial(_dft2_kernel, n2=n2, n2p=n2p, scale=scale),
        grid=(n1 // 16, FN_WIDTH // tn),
        in_specs=[
            pl.BlockSpec((n2p, 16, tn), lambda i, h: (0, i, h)),
            pl.BlockSpec((n2p, 16, tn), lambda i, h: (0, i, h)),
            pl.BlockSpec((n2p, n2p), lambda i, h: (0, 0)),
            pl.BlockSpec((n2p, n2p), lambda i, h: (0, 0)),
        ],
        out_specs=pl.BlockSpec((n2, 16, tn), lambda i, h: (0, i, h)),
        out_shape=jax.ShapeDtypeStruct((n2, n1, FN_WIDTH), BF16),
        scratch_shapes=[pltpu.VMEM((tn // LANE, n2p * 16, LANE), F32)] * 3,
        compiler_params=pltpu.CompilerParams(
            dimension_semantics=("parallel", "parallel"), vmem_limit_bytes=VMEM_LIMIT),
        name="dft_time_2",
    )(yre, yim, jnp.asarray(c2, BF16), jnp.asarray(s2, BF16))
    return out.reshape(t, FN_WIDTH)


def _merge_kernel(x_ref, hb_ref, st_ref, of_ref, ob_ref, za_ref, zf_ref, fr_ref, lig_ref, lib_ref, nw_ref,
                  wg_ref, bg_ref, wa_ref, wf_ref, wo_ref, lg_ref, lb_ref,
                  y_ref, ya_ref, yf_ref, m_ref, *, tn):
    dot = functools.partial(jnp.dot, preferred_element_type=F32)
    nw = nw_ref[...]
    for h in range(HEADS):
        hs = slice(h * HEAD_DIM, (h + 1) * HEAD_DIM)
        o = of_ref[:, hs].astype(F32) + ob_ref[:, hs].astype(F32)
        o = o * lax.rsqrt(jnp.mean(o * o, axis=-1, keepdims=True) + RMS_EPS) * nw
        ya_ref[:, hs] = (o * _silu(za_ref[:, hs].astype(F32))).astype(BF16)
    yf_ref[...] = (fr_ref[...].astype(F32) * _silu(zf_ref[...].astype(F32))).astype(BF16)

    hb = hb_ref[...]
    for j in range(D_MODEL // tn):
        ca = slice(j * tn, (j + 1) * tn)
        cf = slice(D_MODEL + j * tn, D_MODEL + (j + 1) * tn)
        ga = jax.nn.sigmoid(dot(hb, wg_ref[:, ca]) + bg_ref[:, ca])
        gf = jax.nn.sigmoid(dot(hb, wg_ref[:, cf]) + bg_ref[:, cf])
        m_ref[:, ca] = (ga * dot(ya_ref[...], wa_ref[:, ca]) + gf * dot(yf_ref[...], wf_ref[:, ca])).astype(BF16)
    out = dot(m_ref[...], wo_ref[...])

    st = st_ref[...]
    h = (x_ref[...] - st[:, BA_MU:BA_MU + 1]) * st[:, BA_RSTD:BA_RSTD + 1] * lig_ref[...] + lib_ref[...]
    y_ref[...] = _layer_norm(DEEPNORM_ALPHA * h + out, lg_ref[...], lb_ref[...])


def _merge(x, hb, ba, o_f, o_b, p, fr, ln_in_g, ln_in_b, nw, w_g, b_g, w_a, w_f, w_o, ln_g, ln_b, *,
           tm=256, tn=512):
    rows = x.shape[0]
    assert rows % tm == 0 and D_MODEL % tn == 0
    row_spec = lambda w, col=0: pl.BlockSpec((tm, w), lambda i: (i, col))
    return pl.pallas_call(
        functools.partial(_merge_kernel, tn=tn),
        grid=(rows // tm,),
        in_specs=[
            row_spec(D_MODEL), row_spec(D_MODEL), row_spec(LANE),
            row_spec(DN_WIDTH), row_spec(DN_WIDTH),
            row_spec(DN_WIDTH, PCOL_ZA), row_spec(FN_WIDTH, PCOL_ZF),
            row_spec(FN_WIDTH),
            _resident((1, D_MODEL)), _resident((1, D_MODEL)), _resident((1, HEAD_DIM)),
            _resident((D_MODEL, 2 * D_MODEL)), _resident((1, 2 * D_MODEL)),
            _resident((DN_WIDTH, D_MODEL)), _resident((FN_WIDTH, D_MODEL)), _resident((D_MODEL, D_MODEL)),
            _resident((1, D_MODEL)), _resident((1, D_MODEL)),
        ],
        out_specs=pl.BlockSpec((tm, D_MODEL), lambda i: (i, 0)),
        out_shape=jax.ShapeDtypeStruct((rows, D_MODEL), F32),
        scratch_shapes=[
            pltpu.VMEM((tm, DN_WIDTH), BF16),
            pltpu.VMEM((tm, FN_WIDTH), BF16),
            pltpu.VMEM((tm, D_MODEL), BF16),
        ],
        compiler_params=pltpu.CompilerParams(
            dimension_semantics=("parallel",), vmem_limit_bytes=VMEM_LIMIT_RESIDENT),
        name="merge",
    )(x, hb, ba, o_f, o_b, p, p, fr, ln_in_g, ln_in_b, nw, w_g, b_g, w_a, w_f, w_o, ln_g, ln_b)


def _encode(x, p_meta, ba_meta, ab_meta, wts, *, fft_factor=None):
    batch, seq, _ = x.shape
    rows = batch * seq
    t_full = N_META + seq
    x2 = x.reshape(rows, D_MODEL)

    p, ba, hb = _ln_inproj(x2, wts["ln_in_g"], wts["ln_in_b"], wts["w_p"], wts["w_ba"],
                           tm=_row_tile(rows, INPROJ_ROWS))

    tb = DELTA_BLOCK
    assert seq % tb == 0 and seq >= META_BLOCK
    nt = seq // tb
    p4 = p.reshape(batch, nt, tb, P_WIDTH)
    tail = p4[:, :, tb - 2:, :3 * DN_WIDTH]
    head = p4[:, :, :2, :3 * DN_WIDTH]
    meta_tail = jnp.broadcast_to(p_meta[None, None, N_META - 2:, :3 * DN_WIDTH], (batch, 1, 2, 3 * DN_WIDTH))
    prev = jnp.concatenate([meta_tail, tail[:, :-1]], axis=1)
    nxt = jnp.concatenate([head[:, 1:], jnp.zeros((batch, 1, 2, 3 * DN_WIDTH), BF16)], axis=1)
    pad6 = jnp.zeros((batch, nt, 6, 3 * DN_WIDTH), BF16)
    halo = jnp.concatenate([pad6, prev, nxt, pad6], axis=2)
    qkv, kt, bgc, gt = _prep(p, halo, ba, wts["conv_w8"], wts["gpar"], batch=batch, nt=nt, tm=tb)

    mb = META_BLOCK
    p_m = jnp.concatenate([
        jnp.broadcast_to(p_meta[None, :, :3 * DN_WIDTH], (batch, N_META, 3 * DN_WIDTH)),
        p.reshape(batch, seq, P_WIDTH)[:, :mb - N_META, :3 * DN_WIDTH]], axis=1).reshape(batch * mb, 3 * DN_WIDTH)
    ba_m = jnp.concatenate([
        jnp.broadcast_to(ba_meta[None], (batch, N_META, LANE)),
        ba.reshape(batch, seq, LANE)[:, :mb - N_META]], axis=1).reshape(batch * mb, LANE)
    halo_m = jnp.zeros((batch, 1, 16, 3 * DN_WIDTH), BF16)
    qkv_m, kt_m, bgc_m, gt_m = _prep(p_m, halo_m, ba_m, wts["conv_w8"], wts["gpar"],
                                     batch=batch, nt=1, tm=mb, valid_rows=N_META)
    s_zero = jnp.zeros((batch, NPAIR, HEAD_DIM, PAIR), F32)
    _, _, s_meta = _delta(qkv_m, kt_m, bgc_m, gt_m, s_zero, batch=batch, nblk=1, tb=mb)

    o_f, o_b, _ = _delta(qkv, kt, bgc, gt, s_meta, batch=batch, nblk=nt, tb=tb)

    n1, n2 = fft_factor if fft_factor is not None else _fft_factor(t_full)
    if n2 == 1:
        a_x, b_x = _caxis(p, wts["cs"])
        fr = _dft_dense(a_x, b_x, ab_meta[0], ab_meta[1], batch=batch, seq=seq)
    else:
        assert batch == 1
        fr = _dft_four_step(p, p_meta, wts["cs"], seq=seq, factor=(n1, n2))

    y = _merge(x2, hb, ba, o_f, o_b, p, fr, wts["ln_in_g"], wts["ln_in_b"], wts["nw"], wts["w_g"], wts["b_g"],
               wts["w_a"], wts["w_f"], wts["w_o"], wts["ln_g"], wts["ln_b"], tm=_row_tile(rows, MERGE_ROWS))
    return y.reshape(batch, seq, D_MODEL)


def _prepare_weights(ln_in_g, ln_in_b, w_in, b_gate, conv_w, a_log, dt_bias, dn_norm_w,
                     w_proj_a, w_proj_f, w_out, ln_g, ln_b):
    w = w_in[0]
    col_b = 4 * DN_WIDTH
    col_f = col_b + 4 * HEADS
    col_g = col_f + 2 * FN_WIDTH
    w_p = jnp.concatenate([w[:, :col_b], w[:, col_f:col_g]], axis=1).astype(BF16)
    w_ba = jnp.pad(w[:, col_b:col_f], ((0, 0), (0, LANE - 4 * HEADS))).astype(BF16)
    gpar = jnp.zeros((8, LANE), F32)
    gpar = gpar.at[0, BG_G:BG_GC].set(a_log[0].reshape(-1).astype(F32))
    gpar = gpar.at[1, BG_G:BG_GC].set(dt_bias[0].reshape(-1).astype(F32))
    k = np.arange(FN_GROUP_DIM, dtype=np.int64)
    ang = 2.0 * np.pi * ((k[:, None] * k[None, :]) % FN_GROUP_DIM) / FN_GROUP_DIM
    cs = jnp.asarray(np.concatenate([np.cos(ang), np.sin(ang)], axis=1), BF16)
    return {
        "ln_in_g": ln_in_g.reshape(1, D_MODEL).astype(F32),
        "ln_in_b": ln_in_b.reshape(1, D_MODEL).astype(F32),
        "w_p": w_p,
        "w_ba": w_ba,
        "conv_w8": jnp.pad(conv_w[0].astype(F32), ((0, 8 - CONV_K), (0, 0))),
        "gpar": gpar,
        "cs": cs,
        "nw": dn_norm_w[0].reshape(1, HEAD_DIM).astype(F32),
        "w_g": w[:, col_g:].astype(BF16),
        "b_g": b_gate[0].reshape(1, 2 * D_MODEL).astype(F32),
        "w_a": w_proj_a[0].astype(BF16),
        "w_f": w_proj_f[0].astype(BF16),
        "w_o": w_out[0].astype(BF16),
        "ln_g": ln_g[0].reshape(1, D_MODEL).astype(F32),
        "ln_b": ln_b[0].reshape(1, D_MODEL).astype(F32),
    }


def kernel(x_prompt, x_sample, meta_tokens, ln_in_g, ln_in_b, w_in, b_gate, conv_w, a_log, dt_bias, dn_norm_w,
           w_proj_a, w_proj_f, w_out, ln_g, ln_b):
    wts = _prepare_weights(ln_in_g, ln_in_b, w_in, b_gate, conv_w, a_log, dt_bias, dn_norm_w,
                           w_proj_a, w_proj_f, w_out, ln_g, ln_b)
    p_meta, ba_meta, _ = _ln_inproj(meta_tokens.astype(F32), wts["ln_in_g"], wts["ln_in_b"],
                                    wts["w_p"], wts["w_ba"], tm=N_META)
    ab_meta = _caxis(p_meta, wts["cs"])
    y_prompt = _encode(x_prompt, p_meta, ba_meta, ab_meta, wts)
    y_sample = _encode(x_sample, p_meta, ba_meta, ab_meta, wts)
    return (y_prompt, y_sample)
```

```python
import functools
import math

import numpy as np
import jax
import jax.numpy as jnp
from jax import lax
from jax.experimental import pallas as pl
from jax.experimental.pallas import tpu as pltpu

F32 = jnp.float32
BF16 = jnp.bfloat16

D_MODEL = 2048
N_META = 16
HEADS = 8
HEAD_DIM = 128
DN_WIDTH = HEADS * HEAD_DIM
CONV_K = 5
FN_GROUPS = 4
FN_GROUP_DIM = 256
FN_WIDTH = FN_GROUPS * FN_GROUP_DIM
DEPTH = 1
DEEPNORM_ALPHA = (2 * DEPTH) ** 0.25
LN_EPS = 1e-5
RMS_EPS = 1e-6
L2_EPS = 1e-6

LANE = 128
CHUNK = HEAD_DIM
PAIR = 2 * HEAD_DIM
NPAIR = HEADS // 2
INV_BASE = 16
PHASE_A_GROUP = 16
DELTA_BLOCK = 256
META_BLOCK = CHUNK
T2_GROUP = 16
VMEM_LIMIT = 56 * 1024 * 1024
VMEM_LIMIT_RESIDENT = 60 * 1024 * 1024
INPROJ_ROWS = 512
MERGE_ROWS = 256

PCOL_Q, PCOL_K, PCOL_V, PCOL_ZA, PCOL_F, PCOL_ZF = range(6)
P_WIDTH = 6 * 1024
BG_BETA, BG_G, BG_GC, BG_TOT = 0, 16, 32, 48
BA_MU, BA_RSTD = 32, 33
GT_ROWS = 64


def _mm(a, b):
    return jnp.dot(a.astype(BF16), b.astype(BF16), preferred_element_type=F32)


def _layer_norm(x, g, b):
    mu = jnp.mean(x, axis=-1, keepdims=True)
    xc = x - mu
    var = jnp.mean(xc * xc, axis=-1, keepdims=True)
    return xc * lax.rsqrt(var + LN_EPS) * g + b


def _silu(x):
    return x * jax.nn.sigmoid(x)


def _row_tile(rows, cap, mult=16):
    best = None
    for t in range(mult, min(rows, cap) + 1, mult):
        if rows % t == 0:
            best = t
    assert best is not None, (rows, cap, mult)
    return best


def _dot_t(a, bt):
    return lax.dot_general(a, bt, (((1,), (1,)), ((), ())), preferred_element_type=F32)


def _ln_inproj_kernel(x_ref, g_ref, b_ref, w1_ref, w2_ref, wba_ref, p_ref, ba_ref, hb_ref, *, tn):
    x = x_ref[...]
    mu = jnp.mean(x, axis=-1, keepdims=True)
    xc = x - mu
    rstd = lax.rsqrt(jnp.mean(xc * xc, axis=-1, keepdims=True) + LN_EPS)
    hb = (xc * rstd * g_ref[...] + b_ref[...]).astype(BF16)
    hb_ref[...] = hb
    lane = lax.broadcasted_iota(jnp.int32, ba_ref.shape, 1)
    ba = jnp.where(lane < BG_GC, _dot_t(hb, wba_ref[...]), 0.0)
    ba_ref[...] = jnp.where(lane == BA_MU, mu, jnp.where(lane == BA_RSTD, rstd, ba))
    col = 0
    for w_ref in (w1_ref, w2_ref):
        for j in range(w_ref.shape[0] // tn):
            p_ref[:, col:col + tn] = _dot_t(hb, w_ref[j * tn:(j + 1) * tn, :]).astype(p_ref.dtype)
            col += tn


def _resident(shape):
    return pl.BlockSpec(shape, lambda i: (0,) * len(shape), pipeline_mode=pl.Buffered(1))


def _ln_inproj(x, ln_g, ln_b, w1t, w2t, wbat, *, tm, tn=1024):
    rows = x.shape[0]
    assert rows % tm == 0 and w1t.shape[0] % tn == 0 and w2t.shape[0] % tn == 0
    assert w1t.shape[0] + w2t.shape[0] == P_WIDTH and wbat.shape[0] == LANE
    return pl.pallas_call(
        functools.partial(_ln_inproj_kernel, tn=tn),
        grid=(rows // tm,),
        in_specs=[
            pl.BlockSpec((tm, D_MODEL), lambda i: (i, 0)),
            _resident((1, D_MODEL)),
            _resident((1, D_MODEL)),
            _resident(w1t.shape),
            _resident(w2t.shape),
            _resident(wbat.shape),
        ],
        out_specs=[
            pl.BlockSpec((tm, P_WIDTH), lambda i: (i, 0)),
            pl.BlockSpec((tm, LANE), lambda i: (i, 0)),
            pl.BlockSpec((tm, D_MODEL), lambda i: (i, 0)),
        ],
        out_shape=[
            jax.ShapeDtypeStruct((rows, P_WIDTH), BF16),
            jax.ShapeDtypeStruct((rows, LANE), F32),
            jax.ShapeDtypeStruct((rows, D_MODEL), BF16),
        ],
        compiler_params=pltpu.CompilerParams(
            dimension_semantics=("parallel",), vmem_limit_bytes=VMEM_LIMIT_RESIDENT),
        name="ln_inproj",
    )(x, ln_g, ln_b, w1t, w2t, wbat)


def _split3(x):
    hi = x.astype(BF16)
    r1 = x - hi.astype(F32)
    mid = r1.astype(BF16)
    lo = (r1 - mid.astype(F32)).astype(BF16)
    return hi, mid, lo


def _prep_kernel(p_ref, halo_ref, ba_ref, cw_ref, gpar_ref, qkv_ref, kt_ref, bgc_ref, gt_ref, *, tm, valid_rows):
    cw = cw_ref[...]
    row = lax.broadcasted_iota(jnp.int32, (tm, 1), 0)
    if valid_rows is not None:
        keep = row < valid_rows

    shifts = (-2, -1, 1, 2)
    r = lax.broadcasted_iota(jnp.int32, (tm, tm), 0)
    c = lax.broadcasted_iota(jnp.int32, (tm, tm), 1)
    rh = lax.broadcasted_iota(jnp.int32, (8, 16), 0)
    ch = lax.broadcasted_iota(jnp.int32, (8, 16), 1)
    sel = jnp.concatenate([jnp.where(c == r + s, 1.0, 0.0) for s in shifts], axis=0).astype(BF16)
    sel_halo = jnp.concatenate(
        [jnp.where((rh < -s) & (ch == 8 + rh + s) if s < 0 else (rh >= 8 - s) & (ch == rh + s), 1.0, 0.0)
         for s in shifts], axis=0).astype(BF16)

    def shifted(j):
        return jnp.dot(sel, p_ref[:, j * PAIR:(j + 1) * PAIR], preferred_element_type=F32)

    nchunks = 3 * DN_WIDTH // PAIR
    ahead = shifted(0)
    for j in range(nchunks):
        ls = slice(j * PAIR, (j + 1) * PAIR)
        moved = ahead
        if j + 1 < nchunks:
            ahead = shifted(j + 1)
        acc = p_ref[:, ls].astype(F32) * cw[2:3, ls]
        for n, s in enumerate(shifts):
            acc = acc + moved[n * tm:(n + 1) * tm] * cw[2 + s:3 + s, ls]
        edge = jnp.dot(sel_halo, halo_ref[:, ls], preferred_element_type=F32)
        top = acc[0:8]
        bottom = acc[tm - 8:tm]
        for n, s in enumerate(shifts):
            term = edge[8 * n:8 * n + 8] * cw[2 + s:3 + s, ls]
            if s < 0:
                top = top + term
            else:
                bottom = bottom + term
        acc = jnp.concatenate([top, acc[8:tm - 8], bottom], axis=0)
        act = _silu(acc)
        if valid_rows is not None:
            act = jnp.where(keep, act, 0.0)
        if j * PAIR >= 2 * DN_WIDTH:
            qkv_ref[:, ls] = act.astype(qkv_ref.dtype)
            continue
        scale = HEAD_DIM ** -0.5 if j * PAIR < DN_WIDTH else 1.0
        for half in range(PAIR // HEAD_DIM):
            t = act[:, half * HEAD_DIM:(half + 1) * HEAD_DIM]
            tn = t * (lax.rsqrt(jnp.sum(t * t, axis=-1, keepdims=True) + L2_EPS) * scale)
            col = j * PAIR + half * HEAD_DIM
            qkv_ref[:, col:col + HEAD_DIM] = tn.astype(qkv_ref.dtype)
            if col >= DN_WIDTH:
                kt_ref[col - DN_WIDTH:col - DN_WIDTH + HEAD_DIM, :] = tn.T.astype(kt_ref.dtype)

    ba = ba_ref[...]
    gpar = gpar_ref[...]
    lane = lax.broadcasted_iota(jnp.int32, (tm, LANE), 1)
    beta = jax.nn.sigmoid(ba)
    z = ba + gpar[1:2, :]
    softplus = jnp.maximum(z, 0.0) + jnp.log1p(jnp.exp(-jnp.abs(z)))
    g = -jnp.exp(gpar[0:1, :]) * softplus
    is_g = (lane >= BG_G) & (lane < BG_GC)
    g = jnp.where(is_g, g, 0.0)
    beta = jnp.where(lane < BG_G, beta, 0.0)
    if valid_rows is not None:
        g = jnp.where(keep, g, 0.0)
        beta = jnp.where(keep, beta, 0.0)

    r = lax.broadcasted_iota(jnp.int32, (tm, tm), 0)
    c = lax.broadcasted_iota(jnp.int32, (tm, tm), 1)
    same = (r // CHUNK) == (c // CHUNK)
    m_pre = jnp.where(same & (r >= c), 1.0, 0.0).astype(BF16)
    m_suf = jnp.where(same & (r <= c), 1.0, 0.0).astype(BF16)
    pre = jnp.zeros((tm, LANE), F32)
    suf = jnp.zeros((tm, LANE), F32)
    for part in _split3(g):
        pre = pre + jnp.dot(m_pre, part, preferred_element_type=F32)
        suf = suf + jnp.dot(m_suf, part, preferred_element_type=F32)
    tot = pre + suf - g
    fwd_lane = lane < BG_G + HEADS
    gc = jnp.where(fwd_lane, pre, suf)
    out = beta + g + pltpu.roll(gc, BG_GC - BG_G, axis=1) + pltpu.roll(tot, BG_TOT - BG_G, axis=1)
    bgc_ref[...] = out
    gt_ref[...] = out.T[:GT_ROWS, :]


def _prep(p, halo, ba, conv_w8, gpar, *, batch, nt, tm, valid_rows=None):
    rows = batch * nt * tm
    rmap = lambda b, i: (b * nt + i, 0)
    cmap = lambda b, i: (0, b * nt + i)
    return pl.pallas_call(
        functools.partial(_prep_kernel, tm=tm, valid_rows=valid_rows),
        grid=(batch, nt),
        in_specs=[
            pl.BlockSpec((tm, 3 * DN_WIDTH), rmap),
            pl.BlockSpec((None, None, 16, 3 * DN_WIDTH), lambda b, i: (b, i, 0, 0)),
            pl.BlockSpec((tm, LANE), rmap),
            pl.BlockSpec((8, 3 * DN_WIDTH), lambda b, i: (0, 0)),
            pl.BlockSpec((8, LANE), lambda b, i: (0, 0)),
        ],
        out_specs=[
            pl.BlockSpec((tm, 3 * DN_WIDTH), rmap),
            pl.BlockSpec((DN_WIDTH, tm), cmap),
            pl.BlockSpec((tm, LANE), rmap),
            pl.BlockSpec((GT_ROWS, tm), cmap),
        ],
        out_shape=[
            jax.ShapeDtypeStruct((rows, 3 * DN_WIDTH), BF16),
            jax.ShapeDtypeStruct((DN_WIDTH, rows), BF16),
            jax.ShapeDtypeStruct((rows, LANE), F32),
            jax.ShapeDtypeStruct((GT_ROWS, rows), F32),
        ],
        compiler_params=pltpu.CompilerParams(
            dimension_semantics=("parallel", "parallel"), vmem_limit_bytes=VMEM_LIMIT),
        name="prep",
    )(p, halo, ba, conv_w8, gpar)


def _bd(y):
    yb = y.astype(BF16)
    z = jnp.zeros((HEAD_DIM, HEAD_DIM), BF16)
    top = jnp.concatenate([yb[:, :HEAD_DIM], z], axis=1)
    bot = jnp.concatenate([z, yb[:, HEAD_DIM:]], axis=1)
    return jnp.concatenate([top, bot], axis=0)


def _overlay(m, b):
    out = m[0:b]
    for g in range(1, CHUNK // b):
        out = out + m[g * b:(g + 1) * b]
    return out


def _spread(m_ov, same):
    return jnp.where(same, jnp.concatenate([m_ov] * (CHUNK // m_ov.shape[0]), axis=0), 0.0)


def _pair_cols(bg, lane0):
    a = jnp.broadcast_to(bg[:, lane0:lane0 + 1], (CHUNK, HEAD_DIM))
    b = jnp.broadcast_to(bg[:, lane0 + 1:lane0 + 2], (CHUNK, HEAD_DIM))
    return jnp.concatenate([a, b], axis=1)


def _pair_rows(gt, row0, rs):
    return jnp.concatenate([gt[row0:row0 + 1, rs], gt[row0 + 1:row0 + 2, rs]], axis=1)


def _delta_kernel(qf_ref, kf_ref, vf_ref, ktf_ref, bgf_ref, gtf_ref,
                  qb_ref, kb_ref, vb_ref, ktb_ref, bgb_ref, gtb_ref, s0_ref,
                  of_ref, ob_ref, sfin_ref,
                  s_scr, pk_scr, kq_scr, bv_scr, gl_scr, *, nchunk, ndir):
    i = pl.program_id(1)

    @pl.when(i == 0)
    def _():
        s_scr[0] = s0_ref[...]
        s_scr[1] = jnp.zeros_like(s_scr[1])

    row = lax.broadcasted_iota(jnp.int32, (CHUNK, PAIR), 0)
    col = lax.broadcasted_iota(jnp.int32, (CHUNK, PAIR), 1) % HEAD_DIM
    eye2 = jnp.where(row == col, 1.0, 0.0).astype(F32)
    dirs = (
        dict(q=qf_ref, k=kf_ref, v=vf_ref, kt=ktf_ref, bg=bgf_ref, gt=gtf_ref, o=of_ref, dcol=0,
             incl=row >= col, strict=row > col, order=tuple(range(nchunk))),
        dict(q=qb_ref, k=kb_ref, v=vb_ref, kt=ktb_ref, bg=bgb_ref, gt=gtb_ref, o=ob_ref, dcol=HEADS,
             incl=row <= col, strict=row < col, order=tuple(range(nchunk - 1, -1, -1))),
    )
    units = [(d, step, p) for d in range(ndir) for step in range(nchunk) for p in range(NPAIR)]
    if ndir == 1:
        ob_ref[...] = jnp.zeros_like(ob_ref)
    steps = int(math.log2(INV_BASE)) - 1
    same_block = {}
    size = INV_BASE
    while size < CHUNK:
        same_block[size] = (row // size) == (col // size)
        size *= 2
    eye_base = _overlay(eye2, INV_BASE)

    def phase_a(group):
        st = []
        for u, (d, step, p) in group:
            dr = dirs[d]
            c = dr["order"][step]
            rs = slice(c * CHUNK, (c + 1) * CHUNK)
            ls = slice(p * PAIR, (p + 1) * PAIR)
            j0 = dr["dcol"] + 2 * p
            bg = dr["bg"][rs, :]
            gt = dr["gt"]
            beta = _pair_cols(bg, BG_BETA + j0)
            gc = _pair_cols(bg, BG_GC + j0)
            tot = _pair_cols(bg, BG_TOT + j0)
            gr = _pair_rows(gt, BG_GC + j0, rs)
            totr = _pair_rows(gt, BG_TOT + j0, rs)
            decay = jnp.exp(jnp.where(dr["incl"], gc - gr, -1e30))
            kt2 = jnp.concatenate([dr["kt"][p * PAIR:p * PAIR + HEAD_DIM, rs],
                                   dr["kt"][p * PAIR + HEAD_DIM:(p + 1) * PAIR, rs]], axis=1)
            qb = dr["q"][rs, ls]
            kb = dr["k"][rs, ls]
            qk = jnp.dot(jnp.concatenate([qb, kb], axis=0), _bd(kt2), preferred_element_type=F32)
            lm = jnp.where(dr["strict"], beta * qk[CHUNK:] * decay, 0.0)
            egc = jnp.exp(gc)
            kq_scr[u] = jnp.concatenate([(beta * egc) * kb.astype(F32), egc * qb.astype(F32)],
                                        axis=0).astype(BF16)
            bv_scr[u] = beta * dr["v"][rs, ls].astype(F32)
            gl_scr[u] = jnp.exp(tot[0:8, :])
            st.append(dict(lm=lm, pm=qk[:CHUNK] * decay, kdt=kt2.astype(F32) * jnp.exp(totr - gr)))
        b0 = INV_BASE
        for s in st:
            x_full = -jnp.where(same_block[b0], s["lm"], 0.0)
            s["xw"] = _bd(x_full)
            s["x"] = _overlay(x_full, b0)
            s["t"] = eye_base + s["x"]
        for s in st:
            s["p"] = _mm(s["x"], s["xw"])
        for it in range(steps):
            last = it + 1 == steps
            for s in st:
                lhs = s["t"] if last else jnp.concatenate([s["t"], s["p"]], axis=0)
                both = _mm(lhs, _bd(_spread(s["p"], same_block[b0])))
                s["t"] = s["t"] + both[:b0]
                if not last:
                    s["p"] = both[b0:]
        size = b0
        while size < CHUNK:
            top = 2 * size == CHUNK
            join = ~same_block[size] if top else same_block[2 * size] & ~same_block[size]
            for s in st:
                t_full = _spread(s["t"], same_block[size])
                n_full = jnp.where(join, s["lm"], 0.0)
                s["t2"] = t_full if top else _overlay(t_full, 2 * size)
                s["a"] = _mm(n_full if top else _overlay(n_full, 2 * size), _bd(t_full))
            for s in st:
                a_full = s["a"] if top else _spread(s["a"], same_block[2 * size])
                s["t"] = s["t2"] - _mm(s["t2"], _bd(a_full))
            size *= 2
        for (u, _), s in zip(group, st):
            pk_scr[u] = _mm(jnp.concatenate([s["pm"], s["kdt"]], axis=0), _bd(s["t"])).astype(BF16)

    indexed = list(enumerate(units))
    for g0 in range(0, len(indexed), PHASE_A_GROUP):
        phase_a(indexed[g0:g0 + PHASE_A_GROUP])

    for step in range(nchunk):
        active = [(u, un) for u, un in indexed if un[1] == step]
        xs = [jnp.dot(kq_scr[u], _bd(s_scr[d, p]), preferred_element_type=F32) for u, (d, _, p) in active]
        rr = [bv_scr[u] - x[:CHUNK] for (u, _), x in zip(active, xs)]
        for (u, (d, _, p)), x, r in zip(active, xs, rr):
            dr = dirs[d]
            c = dr["order"][step]
            res = jnp.dot(pk_scr[u], _bd(r), preferred_element_type=F32)
            dr["o"][c * CHUNK:(c + 1) * CHUNK, p * PAIR:(p + 1) * PAIR] = (
                x[CHUNK:] + res[:CHUNK]).astype(dr["o"].dtype)
            s_scr[d, p] = gl_scr[u][0:1, :] * s_scr[d, p] + res[CHUNK:]

    @pl.when(i == pl.num_programs(1) - 1)
    def _():
        sfin_ref[...] = s_scr[0]


def _delta(qkv, kt, bgc, gt, s0, *, batch, nblk, tb, ndir=2):
    rows = batch * nblk * tb
    assert tb % CHUNK == 0 and CHUNK == HEAD_DIM
    nchunk = tb // CHUNK
    nunit = ndir * nchunk * NPAIR
    fwd = lambda b, i: b * nblk + i
    bwd = lambda b, i: b * nblk + (nblk - 1 - i)

    def dir_specs(blk):
        return [
            pl.BlockSpec((tb, DN_WIDTH), lambda b, i: (blk(b, i), 0)),
            pl.BlockSpec((tb, DN_WIDTH), lambda b, i: (blk(b, i), 1)),
            pl.BlockSpec((tb, DN_WIDTH), lambda b, i: (blk(b, i), 2)),
            pl.BlockSpec((DN_WIDTH, tb), lambda b, i: (0, blk(b, i))),
            pl.BlockSpec((tb, LANE), lambda b, i: (blk(b, i), 0)),
            pl.BlockSpec((GT_ROWS, tb), lambda b, i: (0, blk(b, i))),
        ]

    sspec = pl.BlockSpec((None, NPAIR, HEAD_DIM, PAIR), lambda b, i: (b, 0, 0, 0))
    return pl.pallas_call(
        functools.partial(_delta_kernel, nchunk=nchunk, ndir=ndir),
        grid=(batch, nblk),
        in_specs=dir_specs(fwd) + dir_specs(bwd) + [sspec],
        out_specs=[
            pl.BlockSpec((tb, DN_WIDTH), lambda b, i: (fwd(b, i), 0)),
            pl.BlockSpec((tb, DN_WIDTH), lambda b, i: (bwd(b, i), 0)),
            sspec,
        ],
        out_shape=[
            jax.ShapeDtypeStruct((rows, DN_WIDTH), BF16),
            jax.ShapeDtypeStruct((rows, DN_WIDTH), BF16),
            jax.ShapeDtypeStruct((batch, NPAIR, HEAD_DIM, PAIR), F32),
        ],
        scratch_shapes=[
            pltpu.VMEM((2, NPAIR, HEAD_DIM, PAIR), F32),
            pltpu.VMEM((nunit, CHUNK + HEAD_DIM, PAIR), BF16),
            pltpu.VMEM((nunit, 2 * CHUNK, PAIR), BF16),
            pltpu.VMEM((nunit, CHUNK, PAIR), F32),
            pltpu.VMEM((nunit, 8, PAIR), F32),
        ],
        compiler_params=pltpu.CompilerParams(
            dimension_semantics=("parallel", "arbitrary"), vmem_limit_bytes=VMEM_LIMIT),
        name="delta",
    )(qkv, qkv, qkv, kt, bgc, gt, qkv, qkv, qkv, kt, bgc, gt, s0)


def _phase_tables(num, den):
    ang = 2.0 * np.pi * (num % den).astype(np.float64) / den
    return np.cos(ang), np.sin(ang)


def _fft_factor(t):
    if t <= 4096:
        return t, 1
    best = None
    for n1 in range(16, t, 16):
        if t % n1:
            continue
        n2 = t // n1
        n2p = -(-n2 // T2_GROUP) * T2_GROUP
        cost = 4 * n1 + 2 * n2p
        if best is None or cost < best[0]:
            best = (cost, n1, n2)
    assert best is not None, t
    return best[1], best[2]


def _caxis_kernel(f_ref, cs_ref, a_ref, b_ref):
    cs = cs_ref[...]
    for g in range(FN_GROUPS):
        sl = slice(g * FN_GROUP_DIM, (g + 1) * FN_GROUP_DIM)
        ab = jnp.dot(f_ref[:, sl], cs, preferred_element_type=F32)
        a_ref[:, sl] = ab[:, :FN_GROUP_DIM].astype(a_ref.dtype)
        b_ref[:, sl] = ab[:, FN_GROUP_DIM:].astype(b_ref.dtype)


def _caxis(p, cs):
    rows = p.shape[0]
    tr = _row_tile(rows, 1024)
    spec = pl.BlockSpec((tr, FN_WIDTH), lambda i: (i, 0))
    return pl.pallas_call(
        _caxis_kernel,
        grid=(rows // tr,),
        in_specs=[pl.BlockSpec((tr, FN_WIDTH), lambda i: (i, PCOL_F)),
                  pl.BlockSpec((FN_GROUP_DIM, 2 * FN_GROUP_DIM), lambda i: (0, 0))],
        out_specs=[spec, spec],
        out_shape=[jax.ShapeDtypeStruct((rows, FN_WIDTH), BF16)] * 2,
        compiler_params=pltpu.CompilerParams(
            dimension_semantics=("parallel",), vmem_limit_bytes=VMEM_LIMIT),
        name="dft_channel",
    )(p, cs)


def _dft_dense_kernel(ax_ref, bx_ref, am_ref, bm_ref, cx_ref, sx_ref, cm_ref, sm_ref, o_ref, *, scale):
    dot = functools.partial(jnp.dot, preferred_element_type=F32)
    y = (dot(cx_ref[...], ax_ref[...]) - dot(sx_ref[...], bx_ref[...])
         + dot(cm_ref[...], am_ref[...]) - dot(sm_ref[...], bm_ref[...]))
    o_ref[...] = (y * scale).astype(o_ref.dtype)


def _dft_dense(a_x, b_x, a_m, b_m, *, batch, seq):
    t = N_META + seq
    scale = 1.0 / math.sqrt(t * FN_GROUP_DIM)
    k = np.arange(seq, dtype=np.int64)[:, None] + N_META
    cx, sx = _phase_tables(k * (np.arange(seq, dtype=np.int64)[None, :] + N_META), t)
    cm, sm = _phase_tables(k * np.arange(N_META, dtype=np.int64)[None, :], t)
    tmi = _row_tile(seq, 512)
    nm = seq // tmi
    xspec = pl.BlockSpec((seq, FN_WIDTH), lambda bb, m: (bb, 0))
    mspec = pl.BlockSpec((N_META, FN_WIDTH), lambda bb, m: (0, 0))
    return pl.pallas_call(
        functools.partial(_dft_dense_kernel, scale=scale),
        grid=(batch, nm),
        in_specs=[xspec, xspec, mspec, mspec,
                  pl.BlockSpec((tmi, seq), lambda bb, m: (m, 0)),
                  pl.BlockSpec((tmi, seq), lambda bb, m: (m, 0)),
                  pl.BlockSpec((tmi, N_META), lambda bb, m: (m, 0)),
                  pl.BlockSpec((tmi, N_META), lambda bb, m: (m, 0))],
        out_specs=pl.BlockSpec((tmi, FN_WIDTH), lambda bb, m: (bb * nm + m, 0)),
        out_shape=jax.ShapeDtypeStruct((batch * seq, FN_WIDTH), BF16),
        compiler_params=pltpu.CompilerParams(
            dimension_semantics=("parallel", "parallel"), vmem_limit_bytes=VMEM_LIMIT),
        name="dft_time_dense",
    )(a_x, b_x, a_m, b_m, jnp.asarray(cx, BF16), jnp.asarray(sx, BF16), jnp.asarray(cm, BF16), jnp.asarray(sm, BF16))


def _dft1_kernel(px_ref, pm_ref, cs_ref, c_ref, s_ref, twr_ref, twi_ref, yre_ref, yim_ref, scr, *, t, n1, n2):
    j = pl.program_id(1)
    halves = FN_GROUP_DIM // LANE

    @pl.when(j == 0)
    def _():
        for c in range(halves):
            ls = slice(c * LANE, (c + 1) * LANE)
            scr[c, 0:N_META, :] = pm_ref[:, ls].astype(F32)
            scr[c, N_META:t, :] = px_ref[:, ls].astype(F32)
            scr[c, t:t + T2_GROUP, :] = jnp.zeros((T2_GROUP, LANE), F32)

    dot = functools.partial(jnp.dot, preferred_element_type=F32)
    cs = cs_ref[...]
    c1 = c_ref[...]
    s1 = s_ref[...]
    xs = [jnp.concatenate([scr[c, pl.ds(j * T2_GROUP + tt, n1, stride=n2), :] for c in range(halves)],
                          axis=1).astype(BF16) for tt in range(T2_GROUP)]
    abs_ = [dot(x, cs) for x in xs]
    aa = [ab[:, :FN_GROUP_DIM].astype(BF16) for ab in abs_]
    bb = [ab[:, FN_GROUP_DIM:].astype(BF16) for ab in abs_]
    yre = [dot(c1, a) - dot(s1, b) for a, b in zip(aa, bb)]
    yim = [-(dot(c1, b) + dot(s1, a)) for a, b in zip(aa, bb)]
    for tt in range(T2_GROUP):
        live = j * T2_GROUP + tt < n2
        wr = twr_ref[tt]
        wi = twi_ref[tt]
        for c in range(halves):
            ls = slice(c * LANE, (c + 1) * LANE)
            re = yre[tt][:, ls]
            im = yim[tt][:, ls]
            yre_ref[tt, :, ls] = jnp.where(live, re * wr - im * wi, 0.0).astype(yre_ref.dtype)
            yim_ref[tt, :, ls] = jnp.where(live, re * wi + im * wr, 0.0).astype(yim_ref.dtype)


def _dft2_kernel(yre_ref, yim_ref, c_ref, s_ref, o_ref, re_scr, im_scr, o_scr, *, n2, n2p, scale):
    ncol = o_ref.shape[-1] // LANE

    def fill(i2, carry):
        rows = pl.ds(pl.multiple_of(i2 * 16, 16), 16)
        for c in range(ncol):
            ls = slice(c * LANE, (c + 1) * LANE)
            re_scr[c, rows, :] = yre_ref[i2, :, ls].astype(F32)
            im_scr[c, rows, :] = yim_ref[i2, :, ls].astype(F32)
        return carry

    lax.fori_loop(0, n2p, fill, 0)
    dot = functools.partial(jnp.dot, preferred_element_type=F32)
    c2 = c_ref[...]
    s2 = s_ref[...]
    for k in range(16):
        gather = lambda scr: jnp.concatenate(
            [scr[c, pl.ds(k, n2p, stride=16), :] for c in range(ncol)], axis=1).astype(BF16)
        y = (dot(c2, gather(re_scr)) + dot(s2, gather(im_scr))) * scale
        for c in range(ncol):
            o_scr[c, pl.ds(k, n2p, stride=16), :] = y[:, c * LANE:(c + 1) * LANE]

    def drain(i2, carry):
        rows = pl.ds(pl.multiple_of(i2 * 16, 16), 16)
        for c in range(ncol):
            o_ref[i2, :, c * LANE:(c + 1) * LANE] = o_scr[c, rows, :].astype(o_ref.dtype)
        return carry

    lax.fori_loop(0, n2, drain, 0)


def _dft_four_step(p, p_meta, cs, *, seq, factor=None):
    t = N_META + seq
    n1, n2 = factor if factor is not None else _fft_factor(t)
    assert n1 * n2 == t and n1 % 16 == 0 and n2 > 1
    n2p = -(-n2 // T2_GROUP) * T2_GROUP
    scale = 1.0 / math.sqrt(t * FN_GROUP_DIM)
    k1 = np.arange(n1, dtype=np.int64) + N_META
    c1, s1 = _phase_tables(k1[:, None] * np.arange(n1, dtype=np.int64)[None, :], n1)
    twc, tws = _phase_tables(np.arange(n2p, dtype=np.int64)[:, None] * k1[None, :], t)
    twr = jnp.asarray(np.broadcast_to(twc[:, :, None], (n2p, n1, LANE)), F32)
    twi = jnp.asarray(np.broadcast_to(-tws[:, :, None], (n2p, n1, LANE)), F32)
    fcol = lambda g, j: (0, PCOL_F * FN_GROUPS + g)
    const = lambda g, j: (0, 0)
    yre, yim = pl.pallas_call(
        functools.partial(_dft1_kernel, t=t, n1=n1, n2=n2),
        grid=(FN_GROUPS, n2p // T2_GROUP),
        in_specs=[
            pl.BlockSpec((seq, FN_GROUP_DIM), fcol),
            pl.BlockSpec((N_META, FN_GROUP_DIM), fcol),
            pl.BlockSpec((FN_GROUP_DIM, 2 * FN_GROUP_DIM), const),
            pl.BlockSpec((n1, n1), const),
            pl.BlockSpec((n1, n1), const),
            pl.BlockSpec((T2_GROUP, n1, LANE), lambda g, j: (j, 0, 0)),
            pl.BlockSpec((T2_GROUP, n1, LANE), lambda g, j: (j, 0, 0)),
        ],
        out_specs=[pl.BlockSpec((T2_GROUP, n1, FN_GROUP_DIM), lambda g, j: (j, 0, g))] * 2,
        out_shape=[jax.ShapeDtypeStruct((n2p, n1, FN_WIDTH), BF16)] * 2,
        scratch_shapes=[pltpu.VMEM((FN_GROUP_DIM // LANE, t + T2_GROUP, LANE), F32)],
        compiler_params=pltpu.CompilerParams(
            dimension_semantics=("parallel", "arbitrary"), vmem_limit_bytes=VMEM_LIMIT),
        name="dft_time_1",
    )(p, p_meta, cs, jnp.asarray(c1, BF16), jnp.asarray(s1, BF16), twr, twi)

    c2 = np.zeros((n2p, n2p))
    s2 = np.zeros((n2p, n2p))
    k2 = np.arange(n2, dtype=np.int64)
    c2[:n2, :n2], s2[:n2, :n2] = _phase_tables(k2[:, None] * k2[None, :], n2)
    tn = 512
    out = pl.pallas_call(
        functools.partial(_dft2_kernel, n2=n2, n2p=n2p, scale=scale),
        grid=(n1 // 16, FN_WIDTH // tn),
        in_specs=[
            pl.BlockSpec((n2p, 16, tn), lambda i, h: (0, i, h)),
            pl.BlockSpec((n2p, 16, tn), lambda i, h: (0, i, h)),
            pl.BlockSpec((n2p, n2p), lambda i, h: (0, 0)),
            pl.BlockSpec((n2p, n2p), lambda i, h: (0, 0)),
        ],
        out_specs=pl.BlockSpec((n2, 16, tn), lambda i, h: (0, i, h)),
        out_shape=jax.ShapeDtypeStruct((n2, n1, FN_WIDTH), BF16),
        scratch_shapes=[pltpu.VMEM((tn // LANE, n2p * 16, LANE), F32)] * 3,
        compiler_params=pltpu.CompilerParams(
            dimension_semantics=("parallel", "parallel"), vmem_limit_bytes=VMEM_LIMIT),
        name="dft_time_2",
    )(yre, yim, jnp.asarray(c2, BF16), jnp.asarray(s2, BF16))
    return out.reshape(t, FN_WIDTH)


def _merge_kernel(x_ref, hb_ref, st_ref, of_ref, ob_ref, za_ref, zf_ref, fr_ref, lig_ref, lib_ref, nw_ref,
                  wg_ref, bg_ref, wa_ref, wf_ref, wo_ref, lg_ref, lb_ref,
                  y_ref, ya_ref, yf_ref, m_ref, *, tn):
    dot = functools.partial(jnp.dot, preferred_element_type=F32)
    nw = nw_ref[...]
    for h in range(HEADS):
        hs = slice(h * HEAD_DIM, (h + 1) * HEAD_DIM)
        o = of_ref[:, hs].astype(F32) + ob_ref[:, hs].astype(F32)
        o = o * lax.rsqrt(jnp.mean(o * o, axis=-1, keepdims=True) + RMS_EPS) * nw
        ya_ref[:, hs] = (o * _silu(za_ref[:, hs].astype(F32))).astype(BF16)
    yf_ref[...] = (fr_ref[...].astype(F32) * _silu(zf_ref[...].astype(F32))).astype(BF16)

    hb = hb_ref[...]
    for j in range(D_MODEL // tn):
        ca = slice(j * tn, (j + 1) * tn)
        cf = slice(D_MODEL + j * tn, D_MODEL + (j + 1) * tn)
        ga = jax.nn.sigmoid(_dot_t(hb, wg_ref[ca, :]) + bg_ref[:, ca])
        gf = jax.nn.sigmoid(_dot_t(hb, wg_ref[cf, :]) + bg_ref[:, cf])
        m_ref[:, ca] = (ga * dot(ya_ref[...], wa_ref[:, ca]) + gf * dot(yf_ref[...], wf_ref[:, ca])).astype(BF16)
    out = dot(m_ref[...], wo_ref[...])

    st = st_ref[...]
    h = (x_ref[...] - st[:, BA_MU:BA_MU + 1]) * st[:, BA_RSTD:BA_RSTD + 1] * lig_ref[...] + lib_ref[...]
    y_ref[...] = _layer_norm(DEEPNORM_ALPHA * h + out, lg_ref[...], lb_ref[...])


def _merge(x, hb, ba, o_f, o_b, p, fr, ln_in_g, ln_in_b, nw, w_g, b_g, w_a, w_f, w_o, ln_g, ln_b, *,
           tm=256, tn=512):
    rows = x.shape[0]
    assert rows % tm == 0 and D_MODEL % tn == 0
    row_spec = lambda w, col=0: pl.BlockSpec((tm, w), lambda i: (i, col))
    return pl.pallas_call(
        functools.partial(_merge_kernel, tn=tn),
        grid=(rows // tm,),
        in_specs=[
            row_spec(D_MODEL), row_spec(D_MODEL), row_spec(LANE),
            row_spec(DN_WIDTH), row_spec(DN_WIDTH),
            row_spec(DN_WIDTH, PCOL_ZA), row_spec(FN_WIDTH, PCOL_ZF),
            row_spec(FN_WIDTH),
            _resident((1, D_MODEL)), _resident((1, D_MODEL)), _resident((1, HEAD_DIM)),
            _resident((2 * D_MODEL, D_MODEL)), _resident((1, 2 * D_MODEL)),
            _resident((DN_WIDTH, D_MODEL)), _resident((FN_WIDTH, D_MODEL)), _resident((D_MODEL, D_MODEL)),
            _resident((1, D_MODEL)), _resident((1, D_MODEL)),
        ],
        out_specs=pl.BlockSpec((tm, D_MODEL), lambda i: (i, 0)),
        out_shape=jax.ShapeDtypeStruct((rows, D_MODEL), F32),
        scratch_shapes=[
            pltpu.VMEM((tm, DN_WIDTH), BF16),
            pltpu.VMEM((tm, FN_WIDTH), BF16),
            pltpu.VMEM((tm, D_MODEL), BF16),
        ],
        compiler_params=pltpu.CompilerParams(
            dimension_semantics=("parallel",), vmem_limit_bytes=VMEM_LIMIT_RESIDENT),
        name="merge",
    )(x, hb, ba, o_f, o_b, p, p, fr, ln_in_g, ln_in_b, nw, w_g, b_g, w_a, w_f, w_o, ln_g, ln_b)


def _encode(x, p_meta, ba_meta, ab_meta, wts, *, fft_factor=None):
    batch, seq, _ = x.shape
    rows = batch * seq
    t_full = N_META + seq
    x2 = x.reshape(rows, D_MODEL)

    p, ba, hb = _ln_inproj(x2, wts["ln_in_g"], wts["ln_in_b"], wts["w1t"], wts["w2t"], wts["wbat"],
                           tm=_row_tile(rows, INPROJ_ROWS))

    tb = DELTA_BLOCK
    assert seq % tb == 0 and seq >= META_BLOCK
    nt = seq // tb
    p4 = p.reshape(batch, nt, tb, P_WIDTH)
    tail = p4[:, :, tb - 2:, :3 * DN_WIDTH]
    head = p4[:, :, :2, :3 * DN_WIDTH]
    meta_tail = jnp.broadcast_to(p_meta[None, None, N_META - 2:, :3 * DN_WIDTH], (batch, 1, 2, 3 * DN_WIDTH))
    prev = jnp.concatenate([meta_tail, tail[:, :-1]], axis=1)
    nxt = jnp.concatenate([head[:, 1:], jnp.zeros((batch, 1, 2, 3 * DN_WIDTH), BF16)], axis=1)
    pad6 = jnp.zeros((batch, nt, 6, 3 * DN_WIDTH), BF16)
    halo = jnp.concatenate([pad6, prev, nxt, pad6], axis=2)
    qkv, kt, bgc, gt = _prep(p, halo, ba, wts["conv_w8"], wts["gpar"], batch=batch, nt=nt, tm=tb)

    mb = META_BLOCK
    p_m = jnp.concatenate([
        jnp.broadcast_to(p_meta[None, :, :3 * DN_WIDTH], (batch, N_META, 3 * DN_WIDTH)),
        p.reshape(batch, seq, P_WIDTH)[:, :mb - N_META, :3 * DN_WIDTH]], axis=1).reshape(batch * mb, 3 * DN_WIDTH)
    ba_m = jnp.concatenate([
        jnp.broadcast_to(ba_meta[None], (batch, N_META, LANE)),
        ba.reshape(batch, seq, LANE)[:, :mb - N_META]], axis=1).reshape(batch * mb, LANE)
    halo_m = jnp.zeros((batch, 1, 16, 3 * DN_WIDTH), BF16)
    qkv_m, kt_m, bgc_m, gt_m = _prep(p_m, halo_m, ba_m, wts["conv_w8"], wts["gpar"],
                                     batch=batch, nt=1, tm=mb, valid_rows=N_META)
    s_zero = jnp.zeros((batch, NPAIR, HEAD_DIM, PAIR), F32)
    _, _, s_meta = _delta(qkv_m, kt_m, bgc_m, gt_m, s_zero, batch=batch, nblk=1, tb=mb, ndir=1)

    o_f, o_b, _ = _delta(qkv, kt, bgc, gt, s_meta, batch=batch, nblk=nt, tb=tb)

    n1, n2 = fft_factor if fft_factor is not None else _fft_factor(t_full)
    if n2 == 1:
        a_x, b_x = _caxis(p, wts["cs"])
        fr = _dft_dense(a_x, b_x, ab_meta[0], ab_meta[1], batch=batch, seq=seq)
    else:
        assert batch == 1
        fr = _dft_four_step(p, p_meta, wts["cs"], seq=seq, factor=(n1, n2))

    y = _merge(x2, hb, ba, o_f, o_b, p, fr, wts["ln_in_g"], wts["ln_in_b"], wts["nw"], wts["w_g"], wts["b_g"],
               wts["w_a"], wts["w_f"], wts["w_o"], wts["ln_g"], wts["ln_b"], tm=_row_tile(rows, MERGE_ROWS))
    return y.reshape(batch, seq, D_MODEL)


def _prepare_weights(ln_in_g, ln_in_b, w_in, b_gate, conv_w, a_log, dt_bias, dn_norm_w,
                     w_proj_a, w_proj_f, w_out, ln_g, ln_b):
    wt = jnp.swapaxes(w_in[0], 0, 1).astype(BF16)
    col_b = 4 * DN_WIDTH
    col_f = col_b + 4 * HEADS
    col_g = col_f + 2 * FN_WIDTH
    gpar = jnp.zeros((8, LANE), F32)
    gpar = gpar.at[0, BG_G:BG_GC].set(a_log[0].reshape(-1).astype(F32))
    gpar = gpar.at[1, BG_G:BG_GC].set(dt_bias[0].reshape(-1).astype(F32))
    k = np.arange(FN_GROUP_DIM, dtype=np.int64)
    ang = 2.0 * np.pi * ((k[:, None] * k[None, :]) % FN_GROUP_DIM) / FN_GROUP_DIM
    cs = jnp.asarray(np.concatenate([np.cos(ang), np.sin(ang)], axis=1), BF16)
    return {
        "ln_in_g": ln_in_g.reshape(1, D_MODEL).astype(F32),
        "ln_in_b": ln_in_b.reshape(1, D_MODEL).astype(F32),
        "w1t": wt[:col_b],
        "wbat": wt[col_b:col_b + LANE],
        "w2t": wt[col_f:col_g],
        "conv_w8": jnp.pad(conv_w[0].astype(F32), ((0, 8 - CONV_K), (0, 0))),
        "gpar": gpar,
        "cs": cs,
        "nw": dn_norm_w[0].reshape(1, HEAD_DIM).astype(F32),
        "w_g": wt[col_g:],
        "b_g": b_gate[0].reshape(1, 2 * D_MODEL).astype(F32),
        "w_a": w_proj_a[0].astype(BF16),
        "w_f": w_proj_f[0].astype(BF16),
        "w_o": w_out[0].astype(BF16),
        "ln_g": ln_g[0].reshape(1, D_MODEL).astype(F32),
        "ln_b": ln_b[0].reshape(1, D_MODEL).astype(F32),
    }


def kernel(x_prompt, x_sample, meta_tokens, ln_in_g, ln_in_b, w_in, b_gate, conv_w, a_log, dt_bias, dn_norm_w,
           w_proj_a, w_proj_f, w_out, ln_g, ln_b):
    wts = _prepare_weights(ln_in_g, ln_in_b, w_in, b_gate, conv_w, a_log, dt_bias, dn_norm_w,
                           w_proj_a, w_proj_f, w_out, ln_g, ln_b)
    p_meta, ba_meta, _ = _ln_inproj(meta_tokens.astype(F32), wts["ln_in_g"], wts["ln_in_b"],
                                    wts["w1t"], wts["w2t"], wts["wbat"], tm=N_META)
    ab_meta = _caxis(p_meta, wts["cs"])
    y_prompt = _encode(x_prompt, p_meta, ba_meta, ab_meta, wts)
    y_sample = _encode(x_sample, p_meta, ba_meta, ab_meta, wts)
    return (y_prompt, y_sample)
```

```python
import functools
import math

import numpy as np
import jax
import jax.numpy as jnp
from jax import lax
from jax.experimental import pallas as pl
from jax.experimental.pallas import tpu as pltpu

F32 = jnp.float32
BF16 = jnp.bfloat16

D_MODEL = 2048
N_META = 16
HEADS = 8
HEAD_DIM = 128
DN_WIDTH = HEADS * HEAD_DIM
CONV_K = 5
FN_GROUPS = 4
FN_GROUP_DIM = 256
FN_WIDTH = FN_GROUPS * FN_GROUP_DIM
DEPTH = 1
DEEPNORM_ALPHA = (2 * DEPTH) ** 0.25
LN_EPS = 1e-5
RMS_EPS = 1e-6
L2_EPS = 1e-6

LANE = 128
CHUNK = HEAD_DIM
PAIR = 2 * HEAD_DIM
NPAIR = HEADS // 2
INV_BASE = 16
PHASE_A_GROUP = 16
DELTA_BLOCK = 256
META_BLOCK = CHUNK
T2_GROUP = 16
VMEM_LIMIT = 56 * 1024 * 1024
VMEM_LIMIT_RESIDENT = 60 * 1024 * 1024
INPROJ_ROWS = 512
MERGE_ROWS = 256

WROW_B = 4 * DN_WIDTH
WROW_F = WROW_B + 4 * HEADS
WROW_G = WROW_F + 2 * FN_WIDTH
PCOL_Q, PCOL_K, PCOL_V, PCOL_ZA, PCOL_F, PCOL_ZF = range(6)
P_WIDTH = 6 * 1024
BG_BETA, BG_G, BG_GC, BG_TOT = 0, 16, 32, 48
BA_MU, BA_RSTD = 32, 33
GT_ROWS = 64


def _mm(a, b):
    return jnp.dot(a.astype(BF16), b.astype(BF16), preferred_element_type=F32)


def _layer_norm(x, g, b):
    mu = jnp.mean(x, axis=-1, keepdims=True)
    xc = x - mu
    var = jnp.mean(xc * xc, axis=-1, keepdims=True)
    return xc * lax.rsqrt(var + LN_EPS) * g + b


def _silu(x):
    return x * jax.nn.sigmoid(x)


def _row_tile(rows, cap, mult=16):
    best = None
    for t in range(mult, min(rows, cap) + 1, mult):
        if rows % t == 0:
            best = t
    assert best is not None, (rows, cap, mult)
    return best


def _dot_t(a, bt):
    return lax.dot_general(a, bt, (((1,), (1,)), ((), ())), preferred_element_type=F32)


def _ln_inproj_kernel(x_ref, g_ref, b_ref, w1_ref, w2_ref, wba_ref, p_ref, ba_ref, hb_ref, *, tn):
    x = x_ref[...]
    mu = jnp.mean(x, axis=-1, keepdims=True)
    xc = x - mu
    rstd = lax.rsqrt(jnp.mean(xc * xc, axis=-1, keepdims=True) + LN_EPS)
    hb = (xc * rstd * g_ref[...] + b_ref[...]).astype(BF16)
    hb_ref[...] = hb
    lane = lax.broadcasted_iota(jnp.int32, ba_ref.shape, 1)
    ba = jnp.where(lane < BG_GC, _dot_t(hb, wba_ref[...]), 0.0)
    ba_ref[...] = jnp.where(lane == BA_MU, mu, jnp.where(lane == BA_RSTD, rstd, ba))
    col = 0
    for w_ref in (w1_ref, w2_ref):
        for j in range(w_ref.shape[0] // tn):
            p_ref[:, col:col + tn] = _dot_t(hb, w_ref[j * tn:(j + 1) * tn, :]).astype(p_ref.dtype)
            col += tn


def _resident(shape):
    return pl.BlockSpec(shape, lambda i: (0,) * len(shape), pipeline_mode=pl.Buffered(1))


def _resident_rows(row0, nrows):
    return pl.BlockSpec((pl.Element(nrows), pl.Element(D_MODEL)), lambda i: (row0, 0), pipeline_mode=pl.Buffered(1))


def _ln_inproj(x, ln_g, ln_b, wt, *, tm, tn=1024):
    rows = x.shape[0]
    assert rows % tm == 0 and WROW_B % tn == 0 and (WROW_G - WROW_F) % tn == 0
    return pl.pallas_call(
        functools.partial(_ln_inproj_kernel, tn=tn),
        grid=(rows // tm,),
        in_specs=[
            pl.BlockSpec((tm, D_MODEL), lambda i: (i, 0)),
            _resident((1, D_MODEL)),
            _resident((1, D_MODEL)),
            _resident_rows(0, WROW_B),
            _resident_rows(WROW_F, WROW_G - WROW_F),
            _resident_rows(WROW_B, LANE),
        ],
        out_specs=[
            pl.BlockSpec((tm, P_WIDTH), lambda i: (i, 0)),
            pl.BlockSpec((tm, LANE), lambda i: (i, 0)),
            pl.BlockSpec((tm, D_MODEL), lambda i: (i, 0)),
        ],
        out_shape=[
            jax.ShapeDtypeStruct((rows, P_WIDTH), BF16),
            jax.ShapeDtypeStruct((rows, LANE), F32),
            jax.ShapeDtypeStruct((rows, D_MODEL), BF16),
        ],
        compiler_params=pltpu.CompilerParams(
            dimension_semantics=("parallel",), vmem_limit_bytes=VMEM_LIMIT_RESIDENT),
        name="ln_inproj",
    )(x, ln_g, ln_b, wt, wt, wt)


def _split3(x):
    hi = x.astype(BF16)
    r1 = x - hi.astype(F32)
    mid = r1.astype(BF16)
    lo = (r1 - mid.astype(F32)).astype(BF16)
    return hi, mid, lo


def _prep_kernel(p_ref, halo_ref, ba_ref, cw_ref, gpar_ref, qkv_ref, kt_ref, bgc_ref, gt_ref, *, tm, valid_rows):
    cw = cw_ref[...]
    row = lax.broadcasted_iota(jnp.int32, (tm, 1), 0)
    if valid_rows is not None:
        keep = row < valid_rows

    shifts = (-2, -1, 1, 2)
    r = lax.broadcasted_iota(jnp.int32, (tm, tm), 0)
    c = lax.broadcasted_iota(jnp.int32, (tm, tm), 1)
    rh = lax.broadcasted_iota(jnp.int32, (8, 16), 0)
    ch = lax.broadcasted_iota(jnp.int32, (8, 16), 1)
    sel = jnp.concatenate([jnp.where(c == r + s, 1.0, 0.0) for s in shifts], axis=0).astype(BF16)
    sel_halo = jnp.concatenate(
        [jnp.where((rh < -s) & (ch == 8 + rh + s) if s < 0 else (rh >= 8 - s) & (ch == rh + s), 1.0, 0.0)
         for s in shifts], axis=0).astype(BF16)

    def shifted(j):
        return jnp.dot(sel, p_ref[:, j * PAIR:(j + 1) * PAIR], preferred_element_type=F32)

    nchunks = 3 * DN_WIDTH // PAIR
    ahead = shifted(0)
    for j in range(nchunks):
        ls = slice(j * PAIR, (j + 1) * PAIR)
        moved = ahead
        if j + 1 < nchunks:
            ahead = shifted(j + 1)
        acc = p_ref[:, ls].astype(F32) * cw[2:3, ls]
        for n, s in enumerate(shifts):
            acc = acc + moved[n * tm:(n + 1) * tm] * cw[2 + s:3 + s, ls]
        edge = jnp.dot(sel_halo, halo_ref[:, ls], preferred_element_type=F32)
        top = acc[0:8]
        bottom = acc[tm - 8:tm]
        for n, s in enumerate(shifts):
            term = edge[8 * n:8 * n + 8] * cw[2 + s:3 + s, ls]
            if s < 0:
                top = top + term
            else:
                bottom = bottom + term
        acc = jnp.concatenate([top, acc[8:tm - 8], bottom], axis=0)
        act = _silu(acc)
        if valid_rows is not None:
            act = jnp.where(keep, act, 0.0)
        if j * PAIR >= 2 * DN_WIDTH:
            qkv_ref[:, ls] = act.astype(qkv_ref.dtype)
            continue
        scale = HEAD_DIM ** -0.5 if j * PAIR < DN_WIDTH else 1.0
        for half in range(PAIR // HEAD_DIM):
            t = act[:, half * HEAD_DIM:(half + 1) * HEAD_DIM]
            tn = t * (lax.rsqrt(jnp.sum(t * t, axis=-1, keepdims=True) + L2_EPS) * scale)
            col = j * PAIR + half * HEAD_DIM
            qkv_ref[:, col:col + HEAD_DIM] = tn.astype(qkv_ref.dtype)
            if col >= DN_WIDTH:
                kt_ref[col - DN_WIDTH:col - DN_WIDTH + HEAD_DIM, :] = tn.T.astype(kt_ref.dtype)

    ba = ba_ref[...]
    gpar = gpar_ref[...]
    lane = lax.broadcasted_iota(jnp.int32, (tm, LANE), 1)
    beta = jax.nn.sigmoid(ba)
    z = ba + gpar[1:2, :]
    softplus = jnp.maximum(z, 0.0) + jnp.log1p(jnp.exp(-jnp.abs(z)))
    g = -jnp.exp(gpar[0:1, :]) * softplus
    is_g = (lane >= BG_G) & (lane < BG_GC)
    g = jnp.where(is_g, g, 0.0)
    beta = jnp.where(lane < BG_G, beta, 0.0)
    if valid_rows is not None:
        g = jnp.where(keep, g, 0.0)
        beta = jnp.where(keep, beta, 0.0)

    r = lax.broadcasted_iota(jnp.int32, (tm, tm), 0)
    c = lax.broadcasted_iota(jnp.int32, (tm, tm), 1)
    same = (r // CHUNK) == (c // CHUNK)
    m_pre = jnp.where(same & (r >= c), 1.0, 0.0).astype(BF16)
    m_suf = jnp.where(same & (r <= c), 1.0, 0.0).astype(BF16)
    pre = jnp.zeros((tm, LANE), F32)
    suf = jnp.zeros((tm, LANE), F32)
    for part in _split3(g):
        pre = pre + jnp.dot(m_pre, part, preferred_element_type=F32)
        suf = suf + jnp.dot(m_suf, part, preferred_element_type=F32)
    tot = pre + suf - g
    fwd_lane = lane < BG_G + HEADS
    gc = jnp.where(fwd_lane, pre, suf)
    out = beta + g + pltpu.roll(gc, BG_GC - BG_G, axis=1) + pltpu.roll(tot, BG_TOT - BG_G, axis=1)
    bgc_ref[...] = out
    gt_ref[...] = out.T[:GT_ROWS, :]


def _prep(p, halo, ba, conv_w8, gpar, *, batch, nt, tm, valid_rows=None):
    rows = batch * nt * tm
    rmap = lambda b, i: (b * nt + i, 0)
    cmap = lambda b, i: (0, b * nt + i)
    return pl.pallas_call(
        functools.partial(_prep_kernel, tm=tm, valid_rows=valid_rows),
        grid=(batch, nt),
        in_specs=[
            pl.BlockSpec((tm, 3 * DN_WIDTH), rmap),
            pl.BlockSpec((None, None, 16, 3 * DN_WIDTH), lambda b, i: (b, i, 0, 0)),
            pl.BlockSpec((tm, LANE), rmap),
            pl.BlockSpec((8, 3 * DN_WIDTH), lambda b, i: (0, 0)),
            pl.BlockSpec((8, LANE), lambda b, i: (0, 0)),
        ],
        out_specs=[
            pl.BlockSpec((tm, 3 * DN_WIDTH), rmap),
            pl.BlockSpec((DN_WIDTH, tm), cmap),
            pl.BlockSpec((tm, LANE), rmap),
            pl.BlockSpec((GT_ROWS, tm), cmap),
        ],
        out_shape=[
            jax.ShapeDtypeStruct((rows, 3 * DN_WIDTH), BF16),
            jax.ShapeDtypeStruct((DN_WIDTH, rows), BF16),
            jax.ShapeDtypeStruct((rows, LANE), F32),
            jax.ShapeDtypeStruct((GT_ROWS, rows), F32),
        ],
        compiler_params=pltpu.CompilerParams(
            dimension_semantics=("parallel", "parallel"), vmem_limit_bytes=VMEM_LIMIT),
        name="prep",
    )(p, halo, ba, conv_w8, gpar)


def _bd(y):
    yb = y.astype(BF16)
    z = jnp.zeros((HEAD_DIM, HEAD_DIM), BF16)
    top = jnp.concatenate([yb[:, :HEAD_DIM], z], axis=1)
    bot = jnp.concatenate([z, yb[:, HEAD_DIM:]], axis=1)
    return jnp.concatenate([top, bot], axis=0)


def _overlay(m, b):
    out = m[0:b]
    for g in range(1, CHUNK // b):
        out = out + m[g * b:(g + 1) * b]
    return out


def _spread(m_ov, same):
    return jnp.where(same, jnp.concatenate([m_ov] * (CHUNK // m_ov.shape[0]), axis=0), 0.0)


def _pair_cols(bg, lane0):
    a = jnp.broadcast_to(bg[:, lane0:lane0 + 1], (CHUNK, HEAD_DIM))
    b = jnp.broadcast_to(bg[:, lane0 + 1:lane0 + 2], (CHUNK, HEAD_DIM))
    return jnp.concatenate([a, b], axis=1)


def _pair_rows(gt, row0, rs):
    return jnp.concatenate([gt[row0:row0 + 1, rs], gt[row0 + 1:row0 + 2, rs]], axis=1)


def _delta_kernel(qf_ref, kf_ref, vf_ref, ktf_ref, bgf_ref, gtf_ref,
                  qb_ref, kb_ref, vb_ref, ktb_ref, bgb_ref, gtb_ref, s0_ref,
                  of_ref, ob_ref, sfin_ref,
                  s_scr, pk_scr, kq_scr, bv_scr, gl_scr, *, nchunk, ndir):
    i = pl.program_id(1)

    @pl.when(i == 0)
    def _():
        s_scr[0] = s0_ref[...]
        s_scr[1] = jnp.zeros_like(s_scr[1])

    row = lax.broadcasted_iota(jnp.int32, (CHUNK, PAIR), 0)
    col = lax.broadcasted_iota(jnp.int32, (CHUNK, PAIR), 1) % HEAD_DIM
    eye2 = jnp.where(row == col, 1.0, 0.0).astype(F32)
    dirs = (
        dict(q=qf_ref, k=kf_ref, v=vf_ref, kt=ktf_ref, bg=bgf_ref, gt=gtf_ref, o=of_ref, dcol=0,
             incl=row >= col, strict=row > col, order=tuple(range(nchunk))),
        dict(q=qb_ref, k=kb_ref, v=vb_ref, kt=ktb_ref, bg=bgb_ref, gt=gtb_ref, o=ob_ref, dcol=HEADS,
             incl=row <= col, strict=row < col, order=tuple(range(nchunk - 1, -1, -1))),
    )
    units = [(d, step, p) for d in range(ndir) for step in range(nchunk) for p in range(NPAIR)]
    if ndir == 1:
        ob_ref[...] = jnp.zeros_like(ob_ref)
    steps = int(math.log2(INV_BASE)) - 1
    same_block = {}
    size = INV_BASE
    while size < CHUNK:
        same_block[size] = (row // size) == (col // size)
        size *= 2
    eye_base = _overlay(eye2, INV_BASE)

    def phase_a(group):
        st = []
        for u, (d, step, p) in group:
            dr = dirs[d]
            c = dr["order"][step]
            rs = slice(c * CHUNK, (c + 1) * CHUNK)
            ls = slice(p * PAIR, (p + 1) * PAIR)
            j0 = dr["dcol"] + 2 * p
            bg = dr["bg"][rs, :]
            gt = dr["gt"]
            beta = _pair_cols(bg, BG_BETA + j0)
            gc = _pair_cols(bg, BG_GC + j0)
            tot = _pair_cols(bg, BG_TOT + j0)
            gr = _pair_rows(gt, BG_GC + j0, rs)
            totr = _pair_rows(gt, BG_TOT + j0, rs)
            decay = jnp.exp(jnp.where(dr["incl"], gc - gr, -1e30))
            kt2 = jnp.concatenate([dr["kt"][p * PAIR:p * PAIR + HEAD_DIM, rs],
                                   dr["kt"][p * PAIR + HEAD_DIM:(p + 1) * PAIR, rs]], axis=1)
            qb = dr["q"][rs, ls]
            kb = dr["k"][rs, ls]
            qk = jnp.dot(jnp.concatenate([qb, kb], axis=0), _bd(kt2), preferred_element_type=F32)
            lm = jnp.where(dr["strict"], beta * qk[CHUNK:] * decay, 0.0)
            egc = jnp.exp(gc)
            kq_scr[u] = jnp.concatenate([(beta * egc) * kb.astype(F32), egc * qb.astype(F32)],
                                        axis=0).astype(BF16)
            bv_scr[u] = beta * dr["v"][rs, ls].astype(F32)
            gl_scr[u] = jnp.exp(tot[0:8, :])
            st.append(dict(lm=lm, pm=qk[:CHUNK] * decay, kdt=kt2.astype(F32) * jnp.exp(totr - gr)))
        b0 = INV_BASE
        for s in st:
            x_full = -jnp.where(same_block[b0], s["lm"], 0.0)
            s["xw"] = _bd(x_full)
            s["x"] = _overlay(x_full, b0)
            s["t"] = eye_base + s["x"]
        for s in st:
            s["p"] = _mm(s["x"], s["xw"])
        for it in range(steps):
            last = it + 1 == steps
            for s in st:
                lhs = s["t"] if last else jnp.concatenate([s["t"], s["p"]], axis=0)
                both = _mm(lhs, _bd(_spread(s["p"], same_block[b0])))
                s["t"] = s["t"] + both[:b0]
                if not last:
                    s["p"] = both[b0:]
        size = b0
        while size < CHUNK:
            top = 2 * size == CHUNK
            join = ~same_block[size] if top else same_block[2 * size] & ~same_block[size]
            for s in st:
                t_full = _spread(s["t"], same_block[size])
                n_full = jnp.where(join, s["lm"], 0.0)
                s["t2"] = t_full if top else _overlay(t_full, 2 * size)
                s["a"] = _mm(n_full if top else _overlay(n_full, 2 * size), _bd(t_full))
            for s in st:
                a_full = s["a"] if top else _spread(s["a"], same_block[2 * size])
                s["t"] = s["t2"] - _mm(s["t2"], _bd(a_full))
            size *= 2
        for (u, _), s in zip(group, st):
            pk_scr[u] = _mm(jnp.concatenate([s["pm"], s["kdt"]], axis=0), _bd(s["t"])).astype(BF16)

    indexed = list(enumerate(units))
    for g0 in range(0, len(indexed), PHASE_A_GROUP):
        phase_a(indexed[g0:g0 + PHASE_A_GROUP])

    for step in range(nchunk):
        active = [(u, un) for u, un in indexed if un[1] == step]
        xs = [jnp.dot(kq_scr[u], _bd(s_scr[d, p]), preferred_element_type=F32) for u, (d, _, p) in active]
        rr = [bv_scr[u] - x[:CHUNK] for (u, _), x in zip(active, xs)]
        for (u, (d, _, p)), x, r in zip(active, xs, rr):
            dr = dirs[d]
            c = dr["order"][step]
            res = jnp.dot(pk_scr[u], _bd(r), preferred_element_type=F32)
            dr["o"][c * CHUNK:(c + 1) * CHUNK, p * PAIR:(p + 1) * PAIR] = (
                x[CHUNK:] + res[:CHUNK]).astype(dr["o"].dtype)
            s_scr[d, p] = gl_scr[u][0:1, :] * s_scr[d, p] + res[CHUNK:]

    @pl.when(i == pl.num_programs(1) - 1)
    def _():
        sfin_ref[...] = s_scr[0]


def _delta(qkv, kt, bgc, gt, s0, *, batch, nblk, tb, ndir=2):
    rows = batch * nblk * tb
    assert tb % CHUNK == 0 and CHUNK == HEAD_DIM
    nchunk = tb // CHUNK
    nunit = ndir * nchunk * NPAIR
    fwd = lambda b, i: b * nblk + i
    bwd = lambda b, i: b * nblk + (nblk - 1 - i)

    def dir_specs(blk):
        return [
            pl.BlockSpec((tb, DN_WIDTH), lambda b, i: (blk(b, i), 0)),
            pl.BlockSpec((tb, DN_WIDTH), lambda b, i: (blk(b, i), 1)),
            pl.BlockSpec((tb, DN_WIDTH), lambda b, i: (blk(b, i), 2)),
            pl.BlockSpec((DN_WIDTH, tb), lambda b, i: (0, blk(b, i))),
            pl.BlockSpec((tb, LANE), lambda b, i: (blk(b, i), 0)),
            pl.BlockSpec((GT_ROWS, tb), lambda b, i: (0, blk(b, i))),
        ]

    sspec = pl.BlockSpec((None, NPAIR, HEAD_DIM, PAIR), lambda b, i: (b, 0, 0, 0))
    return pl.pallas_call(
        functools.partial(_delta_kernel, nchunk=nchunk, ndir=ndir),
        grid=(batch, nblk),
        in_specs=dir_specs(fwd) + dir_specs(bwd) + [sspec],
        out_specs=[
            pl.BlockSpec((tb, DN_WIDTH), lambda b, i: (fwd(b, i), 0)),
            pl.BlockSpec((tb, DN_WIDTH), lambda b, i: (bwd(b, i), 0)),
            sspec,
        ],
        out_shape=[
            jax.ShapeDtypeStruct((rows, DN_WIDTH), BF16),
            jax.ShapeDtypeStruct((rows, DN_WIDTH), BF16),
            jax.ShapeDtypeStruct((batch, NPAIR, HEAD_DIM, PAIR), F32),
        ],
        scratch_shapes=[
            pltpu.VMEM((2, NPAIR, HEAD_DIM, PAIR), F32),
            pltpu.VMEM((nunit, CHUNK + HEAD_DIM, PAIR), BF16),
            pltpu.VMEM((nunit, 2 * CHUNK, PAIR), BF16),
            pltpu.VMEM((nunit, CHUNK, PAIR), F32),
            pltpu.VMEM((nunit, 8, PAIR), F32),
        ],
        compiler_params=pltpu.CompilerParams(
            dimension_semantics=("parallel", "arbitrary"), vmem_limit_bytes=VMEM_LIMIT),
        name="delta",
    )(qkv, qkv, qkv, kt, bgc, gt, qkv, qkv, qkv, kt, bgc, gt, s0)


def _phase_tables(num, den):
    ang = 2.0 * np.pi * (num % den).astype(np.float64) / den
    return np.cos(ang), np.sin(ang)


def _fft_factor(t):
    if t <= 4096:
        return t, 1
    best = None
    for n1 in range(16, t, 16):
        if t % n1:
            continue
        n2 = t // n1
        n2p = -(-n2 // T2_GROUP) * T2_GROUP
        cost = 4 * n1 + 2 * n2p
        if best is None or cost < best[0]:
            best = (cost, n1, n2)
    assert best is not None, t
    return best[1], best[2]


def _caxis_kernel(f_ref, cs_ref, a_ref, b_ref):
    cs = cs_ref[...]
    for g in range(FN_GROUPS):
        sl = slice(g * FN_GROUP_DIM, (g + 1) * FN_GROUP_DIM)
        ab = jnp.dot(f_ref[:, sl], cs, preferred_element_type=F32)
        a_ref[:, sl] = ab[:, :FN_GROUP_DIM].astype(a_ref.dtype)
        b_ref[:, sl] = ab[:, FN_GROUP_DIM:].astype(b_ref.dtype)


def _caxis(p, cs):
    rows = p.shape[0]
    tr = _row_tile(rows, 1024)
    spec = pl.BlockSpec((tr, FN_WIDTH), lambda i: (i, 0))
    return pl.pallas_call(
        _caxis_kernel,
        grid=(rows // tr,),
        in_specs=[pl.BlockSpec((tr, FN_WIDTH), lambda i: (i, PCOL_F)),
                  pl.BlockSpec((FN_GROUP_DIM, 2 * FN_GROUP_DIM), lambda i: (0, 0))],
        out_specs=[spec, spec],
        out_shape=[jax.ShapeDtypeStruct((rows, FN_WIDTH), BF16)] * 2,
        compiler_params=pltpu.CompilerParams(
            dimension_semantics=("parallel",), vmem_limit_bytes=VMEM_LIMIT),
        name="dft_channel",
    )(p, cs)


def _dft_dense_kernel(ax_ref, bx_ref, am_ref, bm_ref, cx_ref, sx_ref, cm_ref, sm_ref, o_ref, *, scale):
    dot = functools.partial(jnp.dot, preferred_element_type=F32)
    y = (dot(cx_ref[...], ax_ref[...]) - dot(sx_ref[...], bx_ref[...])
         + dot(cm_ref[...], am_ref[...]) - dot(sm_ref[...], bm_ref[...]))
    o_ref[...] = (y * scale).astype(o_ref.dtype)


def _dft_dense(a_x, b_x, a_m, b_m, *, batch, seq):
    t = N_META + seq
    scale = 1.0 / math.sqrt(t * FN_GROUP_DIM)
    k = np.arange(seq, dtype=np.int64)[:, None] + N_META
    cx, sx = _phase_tables(k * (np.arange(seq, dtype=np.int64)[None, :] + N_META), t)
    cm, sm = _phase_tables(k * np.arange(N_META, dtype=np.int64)[None, :], t)
    tmi = _row_tile(seq, 512)
    nm = seq // tmi
    xspec = pl.BlockSpec((seq, FN_WIDTH), lambda bb, m: (bb, 0))
    mspec = pl.BlockSpec((N_META, FN_WIDTH), lambda bb, m: (0, 0))
    return pl.pallas_call(
        functools.partial(_dft_dense_kernel, scale=scale),
        grid=(batch, nm),
        in_specs=[xspec, xspec, mspec, mspec,
                  pl.BlockSpec((tmi, seq), lambda bb, m: (m, 0)),
                  pl.BlockSpec((tmi, seq), lambda bb, m: (m, 0)),
                  pl.BlockSpec((tmi, N_META), lambda bb, m: (m, 0)),
                  pl.BlockSpec((tmi, N_META), lambda bb, m: (m, 0))],
        out_specs=pl.BlockSpec((tmi, FN_WIDTH), lambda bb, m: (bb * nm + m, 0)),
        out_shape=jax.ShapeDtypeStruct((batch * seq, FN_WIDTH), BF16),
        compiler_params=pltpu.CompilerParams(
            dimension_semantics=("parallel", "parallel"), vmem_limit_bytes=VMEM_LIMIT),
        name="dft_time_dense",
    )(a_x, b_x, a_m, b_m, jnp.asarray(cx, BF16), jnp.asarray(sx, BF16), jnp.asarray(cm, BF16), jnp.asarray(sm, BF16))


def _dft1_kernel(px_ref, pm_ref, cs_ref, c_ref, s_ref, twr_ref, twi_ref, yre_ref, yim_ref, scr, *, t, n1, n2):
    j = pl.program_id(1)
    halves = FN_GROUP_DIM // LANE

    @pl.when(j == 0)
    def _():
        for c in range(halves):
            ls = slice(c * LANE, (c + 1) * LANE)
            scr[c, 0:N_META, :] = pm_ref[:, ls].astype(F32)
            scr[c, N_META:t, :] = px_ref[:, ls].astype(F32)
            scr[c, t:t + T2_GROUP, :] = jnp.zeros((T2_GROUP, LANE), F32)

    dot = functools.partial(jnp.dot, preferred_element_type=F32)
    cs = cs_ref[...]
    c1 = c_ref[...]
    s1 = s_ref[...]
    xs = [jnp.concatenate([scr[c, pl.ds(j * T2_GROUP + tt, n1, stride=n2), :] for c in range(halves)],
                          axis=1).astype(BF16) for tt in range(T2_GROUP)]
    abs_ = [dot(x, cs) for x in xs]
    aa = [ab[:, :FN_GROUP_DIM].astype(BF16) for ab in abs_]
    bb = [ab[:, FN_GROUP_DIM:].astype(BF16) for ab in abs_]
    yre = [dot(c1, a) - dot(s1, b) for a, b in zip(aa, bb)]
    yim = [-(dot(c1, b) + dot(s1, a)) for a, b in zip(aa, bb)]
    for tt in range(T2_GROUP):
        live = j * T2_GROUP + tt < n2
        wr = twr_ref[tt]
        wi = twi_ref[tt]
        for c in range(halves):
            ls = slice(c * LANE, (c + 1) * LANE)
            re = yre[tt][:, ls]
            im = yim[tt][:, ls]
            yre_ref[tt, :, ls] = jnp.where(live, re * wr - im * wi, 0.0).astype(yre_ref.dtype)
            yim_ref[tt, :, ls] = jnp.where(live, re * wi + im * wr, 0.0).astype(yim_ref.dtype)


def _dft2_kernel(yre_ref, yim_ref, c_ref, s_ref, o_ref, re_scr, im_scr, o_scr, *, n2, n2p, scale):
    ncol = o_ref.shape[-1] // LANE

    def fill(i2, carry):
        rows = pl.ds(pl.multiple_of(i2 * 16, 16), 16)
        for c in range(ncol):
            ls = slice(c * LANE, (c + 1) * LANE)
            re_scr[c, rows, :] = yre_ref[i2, :, ls].astype(F32)
            im_scr[c, rows, :] = yim_ref[i2, :, ls].astype(F32)
        return carry

    lax.fori_loop(0, n2p, fill, 0)
    dot = functools.partial(jnp.dot, preferred_element_type=F32)
    c2 = c_ref[...]
    s2 = s_ref[...]
    for k in range(16):
        gather = lambda scr: jnp.concatenate(
            [scr[c, pl.ds(k, n2p, stride=16), :] for c in range(ncol)], axis=1).astype(BF16)
        y = (dot(c2, gather(re_scr)) + dot(s2, gather(im_scr))) * scale
        for c in range(ncol):
            o_scr[c, pl.ds(k, n2p, stride=16), :] = y[:, c * LANE:(c + 1) * LANE]

    def drain(i2, carry):
        rows = pl.ds(pl.multiple_of(i2 * 16, 16), 16)
        for c in range(ncol):
            o_ref[i2, :, c * LANE:(c + 1) * LANE] = o_scr[c, rows, :].astype(o_ref.dtype)
        return carry

    lax.fori_loop(0, n2, drain, 0)


def _dft_four_step(p, p_meta, cs, *, seq, factor=None):
    t = N_META + seq
    n1, n2 = factor if factor is not None else _fft_factor(t)
    assert n1 * n2 == t and n1 % 16 == 0 and n2 > 1
    n2p = -(-n2 // T2_GROUP) * T2_GROUP
    scale = 1.0 / math.sqrt(t * FN_GROUP_DIM)
    k1 = np.arange(n1, dtype=np.int64) + N_META
    c1, s1 = _phase_tables(k1[:, None] * np.arange(n1, dtype=np.int64)[None, :], n1)
    twc, tws = _phase_tables(np.arange(n2p, dtype=np.int64)[:, None] * k1[None, :], t)
    twr = jnp.asarray(np.broadcast_to(twc[:, :, None], (n2p, n1, LANE)), F32)
    twi = jnp.asarray(np.broadcast_to(-tws[:, :, None], (n2p, n1, LANE)), F32)
    fcol = lambda g, j: (0, PCOL_F * FN_GROUPS + g)
    const = lambda g, j: (0, 0)
    yre, yim = pl.pallas_call(
        functools.partial(_dft1_kernel, t=t, n1=n1, n2=n2),
        grid=(FN_GROUPS, n2p // T2_GROUP),
        in_specs=[
            pl.BlockSpec((seq, FN_GROUP_DIM), fcol),
            pl.BlockSpec((N_META, FN_GROUP_DIM), fcol),
            pl.BlockSpec((FN_GROUP_DIM, 2 * FN_GROUP_DIM), const),
            pl.BlockSpec((n1, n1), const),
            pl.BlockSpec((n1, n1), const),
            pl.BlockSpec((T2_GROUP, n1, LANE), lambda g, j: (j, 0, 0)),
            pl.BlockSpec((T2_GROUP, n1, LANE), lambda g, j: (j, 0, 0)),
        ],
        out_specs=[pl.BlockSpec((T2_GROUP, n1, FN_GROUP_DIM), lambda g, j: (j, 0, g))] * 2,
        out_shape=[jax.ShapeDtypeStruct((n2p, n1, FN_WIDTH), BF16)] * 2,
        scratch_shapes=[pltpu.VMEM((FN_GROUP_DIM // LANE, t + T2_GROUP, LANE), F32)],
        compiler_params=pltpu.CompilerParams(
            dimension_semantics=("parallel", "arbitrary"), vmem_limit_bytes=VMEM_LIMIT),
        name="dft_time_1",
    )(p, p_meta, cs, jnp.asarray(c1, BF16), jnp.asarray(s1, BF16), twr, twi)

    c2 = np.zeros((n2p, n2p))
    s2 = np.zeros((n2p, n2p))
    k2 = np.arange(n2, dtype=np.int64)
    c2[:n2, :n2], s2[:n2, :n2] = _phase_tables(k2[:, None] * k2[None, :], n2)
    tn = 512
    out = pl.pallas_call(
        functools.partial(_dft2_kernel, n2=n2, n2p=n2p, scale=scale),
        grid=(n1 // 16, FN_WIDTH // tn),
        in_specs=[
            pl.BlockSpec((n2p, 16, tn), lambda i, h: (0, i, h)),
            pl.BlockSpec((n2p, 16, tn), lambda i, h: (0, i, h)),
            pl.BlockSpec((n2p, n2p), lambda i, h: (0, 0)),
            pl.BlockSpec((n2p, n2p), lambda i, h: (0, 0)),
        ],
        out_specs=pl.BlockSpec((n2, 16, tn), lambda i, h: (0, i, h)),
        out_shape=jax.ShapeDtypeStruct((n2, n1, FN_WIDTH), BF16),
        scratch_shapes=[pltpu.VMEM((tn // LANE, n2p * 16, LANE), F32)] * 3,
        compiler_params=pltpu.CompilerParams(
            dimension_semantics=("parallel", "parallel"), vmem_limit_bytes=VMEM_LIMIT),
        name="dft_time_2",
    )(yre, yim, jnp.asarray(c2, BF16), jnp.asarray(s2, BF16))
    return out.reshape(t, FN_WIDTH)


def _merge_kernel(x_ref, hb_ref, st_ref, of_ref, ob_ref, za_ref, zf_ref, fr_ref, lig_ref, lib_ref, nw_ref,
                  wg_ref, bg_ref, wa_ref, wf_ref, wo_ref, lg_ref, lb_ref,
                  y_ref, ya_ref, yf_ref, m_ref, *, tn):
    dot = functools.partial(jnp.dot, preferred_element_type=F32)
    nw = nw_ref[...]
    for h in range(HEADS):
        hs = slice(h * HEAD_DIM, (h + 1) * HEAD_DIM)
        o = of_ref[:, hs].astype(F32) + ob_ref[:, hs].astype(F32)
        o = o * lax.rsqrt(jnp.mean(o * o, axis=-1, keepdims=True) + RMS_EPS) * nw
        ya_ref[:, hs] = (o * _silu(za_ref[:, hs].astype(F32))).astype(BF16)
    yf_ref[...] = (fr_ref[...].astype(F32) * _silu(zf_ref[...].astype(F32))).astype(BF16)

    hb = hb_ref[...]
    for j in range(D_MODEL // tn):
        ca = slice(j * tn, (j + 1) * tn)
        cf = slice(D_MODEL + j * tn, D_MODEL + (j + 1) * tn)
        ga = jax.nn.sigmoid(_dot_t(hb, wg_ref[ca, :]) + bg_ref[:, ca])
        gf = jax.nn.sigmoid(_dot_t(hb, wg_ref[cf, :]) + bg_ref[:, cf])
        m_ref[:, ca] = (ga * dot(ya_ref[...], wa_ref[:, ca]) + gf * dot(yf_ref[...], wf_ref[:, ca])).astype(BF16)
    out = dot(m_ref[...], wo_ref[...])

    st = st_ref[...]
    h = (x_ref[...] - st[:, BA_MU:BA_MU + 1]) * st[:, BA_RSTD:BA_RSTD + 1] * lig_ref[...] + lib_ref[...]
    y_ref[...] = _layer_norm(DEEPNORM_ALPHA * h + out, lg_ref[...], lb_ref[...])


def _merge(x, hb, ba, o_f, o_b, p, fr, ln_in_g, ln_in_b, nw, w_g, b_g, w_a, w_f, w_o, ln_g, ln_b, *,
           tm=256, tn=512):
    rows = x.shape[0]
    assert rows % tm == 0 and D_MODEL % tn == 0
    row_spec = lambda w, col=0: pl.BlockSpec((tm, w), lambda i: (i, col))
    return pl.pallas_call(
        functools.partial(_merge_kernel, tn=tn),
        grid=(rows // tm,),
        in_specs=[
            row_spec(D_MODEL), row_spec(D_MODEL), row_spec(LANE),
            row_spec(DN_WIDTH), row_spec(DN_WIDTH),
            row_spec(DN_WIDTH, PCOL_ZA), row_spec(FN_WIDTH, PCOL_ZF),
            row_spec(FN_WIDTH),
            _resident((1, D_MODEL)), _resident((1, D_MODEL)), _resident((1, HEAD_DIM)),
            _resident_rows(WROW_G, 2 * D_MODEL), _resident((1, 2 * D_MODEL)),
            _resident((DN_WIDTH, D_MODEL)), _resident((FN_WIDTH, D_MODEL)), _resident((D_MODEL, D_MODEL)),
            _resident((1, D_MODEL)), _resident((1, D_MODEL)),
        ],
        out_specs=pl.BlockSpec((tm, D_MODEL), lambda i: (i, 0)),
        out_shape=jax.ShapeDtypeStruct((rows, D_MODEL), F32),
        scratch_shapes=[
            pltpu.VMEM((tm, DN_WIDTH), BF16),
            pltpu.VMEM((tm, FN_WIDTH), BF16),
            pltpu.VMEM((tm, D_MODEL), BF16),
        ],
        compiler_params=pltpu.CompilerParams(
            dimension_semantics=("parallel",), vmem_limit_bytes=VMEM_LIMIT_RESIDENT),
        name="merge",
    )(x, hb, ba, o_f, o_b, p, p, fr, ln_in_g, ln_in_b, nw, w_g, b_g, w_a, w_f, w_o, ln_g, ln_b)


def _encode(x, p_meta, ba_meta, ab_meta, wts, *, fft_factor=None):
    batch, seq, _ = x.shape
    rows = batch * seq
    t_full = N_META + seq
    x2 = x.reshape(rows, D_MODEL)

    p, ba, hb = _ln_inproj(x2, wts["ln_in_g"], wts["ln_in_b"], wts["wt"],
                           tm=_row_tile(rows, INPROJ_ROWS))

    tb = DELTA_BLOCK
    assert seq % tb == 0 and seq >= META_BLOCK
    nt = seq // tb
    p4 = p.reshape(batch, nt, tb, P_WIDTH)
    tail = p4[:, :, tb - 2:, :3 * DN_WIDTH]
    head = p4[:, :, :2, :3 * DN_WIDTH]
    meta_tail = jnp.broadcast_to(p_meta[None, None, N_META - 2:, :3 * DN_WIDTH], (batch, 1, 2, 3 * DN_WIDTH))
    prev = jnp.concatenate([meta_tail, tail[:, :-1]], axis=1)
    nxt = jnp.concatenate([head[:, 1:], jnp.zeros((batch, 1, 2, 3 * DN_WIDTH), BF16)], axis=1)
    pad6 = jnp.zeros((batch, nt, 6, 3 * DN_WIDTH), BF16)
    halo = jnp.concatenate([pad6, prev, nxt, pad6], axis=2)
    qkv, kt, bgc, gt = _prep(p, halo, ba, wts["conv_w8"], wts["gpar"], batch=batch, nt=nt, tm=tb)

    mb = META_BLOCK
    p_m = jnp.concatenate([
        jnp.broadcast_to(p_meta[None, :, :3 * DN_WIDTH], (batch, N_META, 3 * DN_WIDTH)),
        p.reshape(batch, seq, P_WIDTH)[:, :mb - N_META, :3 * DN_WIDTH]], axis=1).reshape(batch * mb, 3 * DN_WIDTH)
    ba_m = jnp.concatenate([
        jnp.broadcast_to(ba_meta[None], (batch, N_META, LANE)),
        ba.reshape(batch, seq, LANE)[:, :mb - N_META]], axis=1).reshape(batch * mb, LANE)
    halo_m = jnp.zeros((batch, 1, 16, 3 * DN_WIDTH), BF16)
    qkv_m, kt_m, bgc_m, gt_m = _prep(p_m, halo_m, ba_m, wts["conv_w8"], wts["gpar"],
                                     batch=batch, nt=1, tm=mb, valid_rows=N_META)
    s_zero = jnp.zeros((batch, NPAIR, HEAD_DIM, PAIR), F32)
    _, _, s_meta = _delta(qkv_m, kt_m, bgc_m, gt_m, s_zero, batch=batch, nblk=1, tb=mb, ndir=1)

    o_f, o_b, _ = _delta(qkv, kt, bgc, gt, s_meta, batch=batch, nblk=nt, tb=tb)

    n1, n2 = fft_factor if fft_factor is not None else _fft_factor(t_full)
    if n2 == 1:
        a_x, b_x = _caxis(p, wts["cs"])
        fr = _dft_dense(a_x, b_x, ab_meta[0], ab_meta[1], batch=batch, seq=seq)
    else:
        assert batch == 1
        fr = _dft_four_step(p, p_meta, wts["cs"], seq=seq, factor=(n1, n2))

    y = _merge(x2, hb, ba, o_f, o_b, p, fr, wts["ln_in_g"], wts["ln_in_b"], wts["nw"], wts["wt"], wts["b_g"],
               wts["w_a"], wts["w_f"], wts["w_o"], wts["ln_g"], wts["ln_b"], tm=_row_tile(rows, MERGE_ROWS))
    return y.reshape(batch, seq, D_MODEL)


def _prepare_weights(ln_in_g, ln_in_b, w_in, b_gate, conv_w, a_log, dt_bias, dn_norm_w,
                     w_proj_a, w_proj_f, w_out, ln_g, ln_b):
    gpar = jnp.zeros((8, LANE), F32)
    gpar = gpar.at[0, BG_G:BG_GC].set(a_log[0].reshape(-1).astype(F32))
    gpar = gpar.at[1, BG_G:BG_GC].set(dt_bias[0].reshape(-1).astype(F32))
    k = np.arange(FN_GROUP_DIM, dtype=np.int64)
    ang = 2.0 * np.pi * ((k[:, None] * k[None, :]) % FN_GROUP_DIM) / FN_GROUP_DIM
    cs = jnp.asarray(np.concatenate([np.cos(ang), np.sin(ang)], axis=1), BF16)
    return {
        "ln_in_g": ln_in_g.reshape(1, D_MODEL).astype(F32),
        "ln_in_b": ln_in_b.reshape(1, D_MODEL).astype(F32),
        "wt": jnp.swapaxes(w_in[0], 0, 1).astype(BF16),
        "conv_w8": jnp.pad(conv_w[0].astype(F32), ((0, 8 - CONV_K), (0, 0))),
        "gpar": gpar,
        "cs": cs,
        "nw": dn_norm_w[0].reshape(1, HEAD_DIM).astype(F32),
        "b_g": b_gate[0].reshape(1, 2 * D_MODEL).astype(F32),
        "w_a": w_proj_a[0].astype(BF16),
        "w_f": w_proj_f[0].astype(BF16),
        "w_o": w_out[0].astype(BF16),
        "ln_g": ln_g[0].reshape(1, D_MODEL).astype(F32),
        "ln_b": ln_b[0].reshape(1, D_MODEL).astype(F32),
    }


def kernel(x_prompt, x_sample, meta_tokens, ln_in_g, ln_in_b, w_in, b_gate, conv_w, a_log, dt_bias, dn_norm_w,
           w_proj_a, w_proj_f, w_out, ln_g, ln_b):
    wts = _prepare_weights(ln_in_g, ln_in_b, w_in, b_gate, conv_w, a_log, dt_bias, dn_norm_w,
                           w_proj_a, w_proj_f, w_out, ln_g, ln_b)
    p_meta, ba_meta, _ = _ln_inproj(meta_tokens.astype(F32), wts["ln_in_g"], wts["ln_in_b"],
                                    wts["wt"], tm=N_META)
    ab_meta = _caxis(p_meta, wts["cs"])
    y_prompt = _encode(x_prompt, p_meta, ba_meta, ab_meta, wts)
    y_sample = _encode(x_sample, p_meta, ba_meta, ab_meta, wts)
    return (y_prompt, y_sample)
```

```python
import functools
import math

import numpy as np
import jax
import jax.numpy as jnp
from jax import lax
from jax.experimental import pallas as pl
from jax.experimental.pallas import tpu as pltpu

F32 = jnp.float32
BF16 = jnp.bfloat16

D_MODEL = 2048
N_META = 16
HEADS = 8
HEAD_DIM = 128
DN_WIDTH = HEADS * HEAD_DIM
CONV_K = 5
FN_GROUPS = 4
FN_GROUP_DIM = 256
FN_WIDTH = FN_GROUPS * FN_GROUP_DIM
DEPTH = 1
DEEPNORM_ALPHA = (2 * DEPTH) ** 0.25
LN_EPS = 1e-5
RMS_EPS = 1e-6
L2_EPS = 1e-6

LANE = 128
CHUNK = HEAD_DIM
PAIR = 2 * HEAD_DIM
NPAIR = HEADS // 2
INV_BASE = 16
PHASE_A_GROUP = 16
DELTA_BLOCK = 256
META_BLOCK = CHUNK
T2_GROUP = 16
VMEM_LIMIT = 56 * 1024 * 1024
VMEM_LIMIT_RESIDENT = 60 * 1024 * 1024
INPROJ_ROWS = 512
MERGE_ROWS = 256

WROW_B = 4 * DN_WIDTH
WROW_F = WROW_B + 4 * HEADS
WROW_G = WROW_F + 2 * FN_WIDTH
PCOL_Q, PCOL_K, PCOL_V, PCOL_ZA, PCOL_F, PCOL_ZF = range(6)
P_WIDTH = 6 * 1024
BG_BETA, BG_G, BG_GC, BG_TOT = 0, 16, 32, 48
BA_MU, BA_RSTD = 32, 33
GT_ROWS = 64


def _mm(a, b):
    return jnp.dot(a.astype(BF16), b.astype(BF16), preferred_element_type=F32)


def _layer_norm(x, g, b):
    mu = jnp.mean(x, axis=-1, keepdims=True)
    xc = x - mu
    var = jnp.mean(xc * xc, axis=-1, keepdims=True)
    return xc * lax.rsqrt(var + LN_EPS) * g + b


def _silu(x):
    return x * jax.nn.sigmoid(x)


def _row_tile(rows, cap, mult=16):
    best = None
    for t in range(mult, min(rows, cap) + 1, mult):
        if rows % t == 0:
            best = t
    assert best is not None, (rows, cap, mult)
    return best


def _dot_t(a, bt):
    return lax.dot_general(a, bt, (((1,), (1,)), ((), ())), preferred_element_type=F32)


def _ln_inproj_kernel(x_ref, g_ref, b_ref, w1_ref, w2_ref, wba_ref, p_ref, ba_ref, hb_ref, *, tn):
    x = x_ref[...]
    mu = jnp.mean(x, axis=-1, keepdims=True)
    xc = x - mu
    rstd = lax.rsqrt(jnp.mean(xc * xc, axis=-1, keepdims=True) + LN_EPS)
    hb = (xc * rstd * g_ref[...] + b_ref[...]).astype(BF16)
    hb_ref[...] = hb
    lane = lax.broadcasted_iota(jnp.int32, ba_ref.shape, 1)
    ba = jnp.where(lane < BG_GC, _dot_t(hb, wba_ref[...]), 0.0)
    ba_ref[...] = jnp.where(lane == BA_MU, mu, jnp.where(lane == BA_RSTD, rstd, ba))
    col = 0
    for w_ref in (w1_ref, w2_ref):
        for j in range(w_ref.shape[0] // tn):
            p_ref[:, col:col + tn] = _dot_t(hb, w_ref[j * tn:(j + 1) * tn, :]).astype(p_ref.dtype)
            col += tn


def _resident(shape):
    return pl.BlockSpec(shape, lambda i: (0,) * len(shape), pipeline_mode=pl.Buffered(1))


def _resident_rows(row0, nrows):
    return pl.BlockSpec((pl.Element(nrows), pl.Element(D_MODEL)), lambda i: (row0, 0), pipeline_mode=pl.Buffered(1))


def _ln_inproj(x, ln_g, ln_b, wt, *, tm, tn=1024):
    rows = x.shape[0]
    assert rows % tm == 0 and WROW_B % tn == 0 and (WROW_G - WROW_F) % tn == 0
    return pl.pallas_call(
        functools.partial(_ln_inproj_kernel, tn=tn),
        grid=(rows // tm,),
        in_specs=[
            pl.BlockSpec((tm, D_MODEL), lambda i: (i, 0)),
            _resident((1, D_MODEL)),
            _resident((1, D_MODEL)),
            _resident_rows(0, WROW_B),
            _resident_rows(WROW_F, WROW_G - WROW_F),
            _resident_rows(WROW_B, LANE),
        ],
        out_specs=[
            pl.BlockSpec((tm, P_WIDTH), lambda i: (i, 0)),
            pl.BlockSpec((tm, LANE), lambda i: (i, 0)),
            pl.BlockSpec((tm, D_MODEL), lambda i: (i, 0)),
        ],
        out_shape=[
            jax.ShapeDtypeStruct((rows, P_WIDTH), BF16),
            jax.ShapeDtypeStruct((rows, LANE), F32),
            jax.ShapeDtypeStruct((rows, D_MODEL), BF16),
        ],
        compiler_params=pltpu.CompilerParams(
            dimension_semantics=("parallel",), vmem_limit_bytes=VMEM_LIMIT_RESIDENT),
        name="ln_inproj",
    )(x, ln_g, ln_b, wt, wt, wt)


def _split3(x):
    hi = x.astype(BF16)
    r1 = x - hi.astype(F32)
    mid = r1.astype(BF16)
    lo = (r1 - mid.astype(F32)).astype(BF16)
    return hi, mid, lo


def _prep_kernel(p_ref, halo_ref, ba_ref, cw_ref, gpar_ref, qkv_ref, kt_ref, bgc_ref, gt_ref, *, tm, valid_rows):
    cw = cw_ref[...]
    row = lax.broadcasted_iota(jnp.int32, (tm, 1), 0)
    if valid_rows is not None:
        keep = row < valid_rows

    shifts = (-2, -1, 1, 2)
    r = lax.broadcasted_iota(jnp.int32, (tm, tm), 0)
    c = lax.broadcasted_iota(jnp.int32, (tm, tm), 1)
    rh = lax.broadcasted_iota(jnp.int32, (8, 16), 0)
    ch = lax.broadcasted_iota(jnp.int32, (8, 16), 1)
    sel = jnp.concatenate([jnp.where(c == r + s, 1.0, 0.0) for s in shifts], axis=0).astype(BF16)
    sel_halo = jnp.concatenate(
        [jnp.where((rh < -s) & (ch == 8 + rh + s) if s < 0 else (rh >= 8 - s) & (ch == rh + s), 1.0, 0.0)
         for s in shifts], axis=0).astype(BF16)

    def shifted(j):
        return jnp.dot(sel, p_ref[:, j * PAIR:(j + 1) * PAIR], preferred_element_type=F32)

    nchunks = 3 * DN_WIDTH // PAIR
    ahead = shifted(0)
    for j in range(nchunks):
        ls = slice(j * PAIR, (j + 1) * PAIR)
        moved = ahead
        if j + 1 < nchunks:
            ahead = shifted(j + 1)
        acc = p_ref[:, ls].astype(F32) * cw[2:3, ls]
        for n, s in enumerate(shifts):
            acc = acc + moved[n * tm:(n + 1) * tm] * cw[2 + s:3 + s, ls]
        edge = jnp.dot(sel_halo, halo_ref[:, ls], preferred_element_type=F32)
        top = acc[0:8]
        bottom = acc[tm - 8:tm]
        for n, s in enumerate(shifts):
            term = edge[8 * n:8 * n + 8] * cw[2 + s:3 + s, ls]
            if s < 0:
                top = top + term
            else:
                bottom = bottom + term
        acc = jnp.concatenate([top, acc[8:tm - 8], bottom], axis=0)
        act = _silu(acc)
        if valid_rows is not None:
            act = jnp.where(keep, act, 0.0)
        if j * PAIR >= 2 * DN_WIDTH:
            qkv_ref[:, ls] = act.astype(qkv_ref.dtype)
            continue
        scale = HEAD_DIM ** -0.5 if j * PAIR < DN_WIDTH else 1.0
        for half in range(PAIR // HEAD_DIM):
            t = act[:, half * HEAD_DIM:(half + 1) * HEAD_DIM]
            tn = t * (lax.rsqrt(jnp.sum(t * t, axis=-1, keepdims=True) + L2_EPS) * scale)
            col = j * PAIR + half * HEAD_DIM
            qkv_ref[:, col:col + HEAD_DIM] = tn.astype(qkv_ref.dtype)
            if col >= DN_WIDTH:
                kt_ref[col - DN_WIDTH:col - DN_WIDTH + HEAD_DIM, :] = tn.T.astype(kt_ref.dtype)

    ba = ba_ref[...]
    gpar = gpar_ref[...]
    lane = lax.broadcasted_iota(jnp.int32, (tm, LANE), 1)
    beta = jax.nn.sigmoid(ba)
    z = ba + gpar[1:2, :]
    softplus = jnp.maximum(z, 0.0) + jnp.log1p(jnp.exp(-jnp.abs(z)))
    g = -jnp.exp(gpar[0:1, :]) * softplus
    is_g = (lane >= BG_G) & (lane < BG_GC)
    g = jnp.where(is_g, g, 0.0)
    beta = jnp.where(lane < BG_G, beta, 0.0)
    if valid_rows is not None:
        g = jnp.where(keep, g, 0.0)
        beta = jnp.where(keep, beta, 0.0)

    r = lax.broadcasted_iota(jnp.int32, (tm, tm), 0)
    c = lax.broadcasted_iota(jnp.int32, (tm, tm), 1)
    same = (r // CHUNK) == (c // CHUNK)
    m_pre = jnp.where(same & (r >= c), 1.0, 0.0).astype(BF16)
    m_suf = jnp.where(same & (r <= c), 1.0, 0.0).astype(BF16)
    pre = jnp.zeros((tm, LANE), F32)
    suf = jnp.zeros((tm, LANE), F32)
    for part in _split3(g):
        pre = pre + jnp.dot(m_pre, part, preferred_element_type=F32)
        suf = suf + jnp.dot(m_suf, part, preferred_element_type=F32)
    tot = pre + suf - g
    fwd_lane = lane < BG_G + HEADS
    gc = jnp.where(fwd_lane, pre, suf)
    out = beta + g + pltpu.roll(gc, BG_GC - BG_G, axis=1) + pltpu.roll(tot, BG_TOT - BG_G, axis=1)
    bgc_ref[...] = out
    gt_ref[...] = out.T[:GT_ROWS, :]


def _prep(p, halo, ba, conv_w8, gpar, *, batch, nt, tm, valid_rows=None):
    rows = batch * nt * tm
    rmap = lambda b, i: (b * nt + i, 0)
    cmap = lambda b, i: (0, b * nt + i)
    return pl.pallas_call(
        functools.partial(_prep_kernel, tm=tm, valid_rows=valid_rows),
        grid=(batch, nt),
        in_specs=[
            pl.BlockSpec((tm, 3 * DN_WIDTH), rmap),
            pl.BlockSpec((None, None, 16, 3 * DN_WIDTH), lambda b, i: (b, i, 0, 0)),
            pl.BlockSpec((tm, LANE), rmap),
            pl.BlockSpec((8, 3 * DN_WIDTH), lambda b, i: (0, 0)),
            pl.BlockSpec((8, LANE), lambda b, i: (0, 0)),
        ],
        out_specs=[
            pl.BlockSpec((tm, 3 * DN_WIDTH), rmap),
            pl.BlockSpec((DN_WIDTH, tm), cmap),
            pl.BlockSpec((tm, LANE), rmap),
            pl.BlockSpec((GT_ROWS, tm), cmap),
        ],
        out_shape=[
            jax.ShapeDtypeStruct((rows, 3 * DN_WIDTH), BF16),
            jax.ShapeDtypeStruct((DN_WIDTH, rows), BF16),
            jax.ShapeDtypeStruct((rows, LANE), F32),
            jax.ShapeDtypeStruct((GT_ROWS, rows), F32),
        ],
        compiler_params=pltpu.CompilerParams(
            dimension_semantics=("parallel", "parallel"), vmem_limit_bytes=VMEM_LIMIT),
        name="prep",
    )(p, halo, ba, conv_w8, gpar)


def _split2(x):
    hi = x.astype(BF16)
    return hi, (x - hi.astype(F32)).astype(BF16)


def _bd(y):
    yb = y.astype(BF16)
    z = jnp.zeros((HEAD_DIM, HEAD_DIM), BF16)
    top = jnp.concatenate([yb[:, :HEAD_DIM], z], axis=1)
    bot = jnp.concatenate([z, yb[:, HEAD_DIM:]], axis=1)
    return jnp.concatenate([top, bot], axis=0)


def _overlay(m, b):
    out = m[0:b]
    for g in range(1, CHUNK // b):
        out = out + m[g * b:(g + 1) * b]
    return out


def _spread(m_ov, same):
    return jnp.where(same, jnp.concatenate([m_ov] * (CHUNK // m_ov.shape[0]), axis=0), 0.0)


def _pair_cols(bg, lane0):
    a = jnp.broadcast_to(bg[:, lane0:lane0 + 1], (CHUNK, HEAD_DIM))
    b = jnp.broadcast_to(bg[:, lane0 + 1:lane0 + 2], (CHUNK, HEAD_DIM))
    return jnp.concatenate([a, b], axis=1)


def _pair_rows(gt, row0, rs):
    return jnp.concatenate([gt[row0:row0 + 1, rs], gt[row0 + 1:row0 + 2, rs]], axis=1)


def _delta_kernel(qf_ref, kf_ref, vf_ref, ktf_ref, bgf_ref, gtf_ref,
                  qb_ref, kb_ref, vb_ref, ktb_ref, bgb_ref, gtb_ref, s0_ref,
                  of_ref, ob_ref, sfin_ref,
                  s_scr, pk_scr, kq_scr, bv_scr, gl_scr, *, nchunk, ndir):
    i = pl.program_id(1)

    @pl.when(i == 0)
    def _():
        s_scr[0] = s0_ref[...]
        s_scr[1] = jnp.zeros_like(s_scr[1])

    row = lax.broadcasted_iota(jnp.int32, (CHUNK, PAIR), 0)
    col = lax.broadcasted_iota(jnp.int32, (CHUNK, PAIR), 1) % HEAD_DIM
    eye2 = jnp.where(row == col, 1.0, 0.0).astype(F32)
    dirs = (
        dict(q=qf_ref, k=kf_ref, v=vf_ref, kt=ktf_ref, bg=bgf_ref, gt=gtf_ref, o=of_ref, dcol=0,
             incl=row >= col, strict=row > col, order=tuple(range(nchunk))),
        dict(q=qb_ref, k=kb_ref, v=vb_ref, kt=ktb_ref, bg=bgb_ref, gt=gtb_ref, o=ob_ref, dcol=HEADS,
             incl=row <= col, strict=row < col, order=tuple(range(nchunk - 1, -1, -1))),
    )
    units = [(d, step, p) for d in range(ndir) for step in range(nchunk) for p in range(NPAIR)]
    if ndir == 1:
        ob_ref[...] = jnp.zeros_like(ob_ref)
    steps = int(math.log2(INV_BASE)) - 1
    same_block = {}
    size = INV_BASE
    while size < CHUNK:
        same_block[size] = (row // size) == (col // size)
        size *= 2
    eye_base = _overlay(eye2, INV_BASE)

    def phase_a(group):
        st = []
        for u, (d, step, p) in group:
            dr = dirs[d]
            c = dr["order"][step]
            rs = slice(c * CHUNK, (c + 1) * CHUNK)
            ls = slice(p * PAIR, (p + 1) * PAIR)
            j0 = dr["dcol"] + 2 * p
            bg = dr["bg"][rs, :]
            gt = dr["gt"]
            beta = _pair_cols(bg, BG_BETA + j0)
            gc = _pair_cols(bg, BG_GC + j0)
            tot = _pair_cols(bg, BG_TOT + j0)
            gr = _pair_rows(gt, BG_GC + j0, rs)
            totr = _pair_rows(gt, BG_TOT + j0, rs)
            decay = jnp.exp(jnp.where(dr["incl"], gc - gr, -1e30))
            kt2 = jnp.concatenate([dr["kt"][p * PAIR:p * PAIR + HEAD_DIM, rs],
                                   dr["kt"][p * PAIR + HEAD_DIM:(p + 1) * PAIR, rs]], axis=1)
            qb = dr["q"][rs, ls]
            kb = dr["k"][rs, ls]
            qk = jnp.dot(jnp.concatenate([qb, kb], axis=0), _bd(kt2), preferred_element_type=F32)
            lm = jnp.where(dr["strict"], beta * qk[CHUNK:] * decay, 0.0)
            egc = jnp.exp(gc)
            kq_scr[u] = jnp.concatenate([(beta * egc) * kb.astype(F32), egc * qb.astype(F32)],
                                        axis=0).astype(BF16)
            bv_scr[u] = beta * dr["v"][rs, ls].astype(F32)
            gl_scr[u] = jnp.exp(tot[0:8, :])
            st.append(dict(lm=lm, pm=qk[:CHUNK] * decay, kdt=kt2.astype(F32) * jnp.exp(totr - gr)))
        b0 = INV_BASE

        def base_product(lhs, rhs_ov):
            lh, ll = _split2(lhs)
            rh, rl = _split2(rhs_ov)
            wh = _bd(_spread(rh, same_block[b0]))
            wl = _bd(_spread(rl, same_block[b0]))
            dot = functools.partial(jnp.dot, preferred_element_type=F32)
            return dot(lh, wh) + (dot(lh, wl) + dot(ll, wh))

        for s in st:
            s["x"] = _overlay(-jnp.where(same_block[b0], s["lm"], 0.0), b0)
            s["t"] = eye_base + s["x"]
        for s in st:
            s["p"] = base_product(s["x"], s["x"])
        for it in range(steps):
            last = it + 1 == steps
            for s in st:
                lhs = s["t"] if last else jnp.concatenate([s["t"], s["p"]], axis=0)
                both = base_product(lhs, s["p"])
                s["t"] = s["t"] + both[:b0]
                if not last:
                    s["p"] = both[b0:]
        size = b0
        while size < CHUNK:
            top = 2 * size == CHUNK
            join = ~same_block[size] if top else same_block[2 * size] & ~same_block[size]
            for s in st:
                t_full = _spread(s["t"], same_block[size])
                n_full = jnp.where(join, s["lm"], 0.0)
                s["t2"] = t_full if top else _overlay(t_full, 2 * size)
                s["a"] = _mm(n_full if top else _overlay(n_full, 2 * size), _bd(t_full))
            for s in st:
                a_full = s["a"] if top else _spread(s["a"], same_block[2 * size])
                s["t"] = s["t2"] - _mm(s["t2"], _bd(a_full))
            size *= 2
        for (u, _), s in zip(group, st):
            pk_scr[u] = _mm(jnp.concatenate([s["pm"], s["kdt"]], axis=0), _bd(s["t"])).astype(BF16)

    indexed = list(enumerate(units))
    for g0 in range(0, len(indexed), PHASE_A_GROUP):
        phase_a(indexed[g0:g0 + PHASE_A_GROUP])

    for step in range(nchunk):
        active = [(u, un) for u, un in indexed if un[1] == step]
        xs = [jnp.dot(kq_scr[u], _bd(s_scr[d, p]), preferred_element_type=F32) for u, (d, _, p) in active]
        rr = [bv_scr[u] - x[:CHUNK] for (u, _), x in zip(active, xs)]
        for (u, (d, _, p)), x, r in zip(active, xs, rr):
            dr = dirs[d]
            c = dr["order"][step]
            res = jnp.dot(pk_scr[u], _bd(r), preferred_element_type=F32)
            dr["o"][c * CHUNK:(c + 1) * CHUNK, p * PAIR:(p + 1) * PAIR] = (
                x[CHUNK:] + res[:CHUNK]).astype(dr["o"].dtype)
            s_scr[d, p] = gl_scr[u][0:1, :] * s_scr[d, p] + res[CHUNK:]

    @pl.when(i == pl.num_programs(1) - 1)
    def _():
        sfin_ref[...] = s_scr[0]


def _delta(qkv, kt, bgc, gt, s0, *, batch, nblk, tb, ndir=2):
    rows = batch * nblk * tb
    assert tb % CHUNK == 0 and CHUNK == HEAD_DIM
    nchunk = tb // CHUNK
    nunit = ndir * nchunk * NPAIR
    fwd = lambda b, i: b * nblk + i
    bwd = lambda b, i: b * nblk + (nblk - 1 - i)

    def dir_specs(blk):
        return [
            pl.BlockSpec((tb, DN_WIDTH), lambda b, i: (blk(b, i), 0)),
            pl.BlockSpec((tb, DN_WIDTH), lambda b, i: (blk(b, i), 1)),
            pl.BlockSpec((tb, DN_WIDTH), lambda b, i: (blk(b, i), 2)),
            pl.BlockSpec((DN_WIDTH, tb), lambda b, i: (0, blk(b, i))),
            pl.BlockSpec((tb, LANE), lambda b, i: (blk(b, i), 0)),
            pl.BlockSpec((GT_ROWS, tb), lambda b, i: (0, blk(b, i))),
        ]

    sspec = pl.BlockSpec((None, NPAIR, HEAD_DIM, PAIR), lambda b, i: (b, 0, 0, 0))
    return pl.pallas_call(
        functools.partial(_delta_kernel, nchunk=nchunk, ndir=ndir),
        grid=(batch, nblk),
        in_specs=dir_specs(fwd) + dir_specs(bwd) + [sspec],
        out_specs=[
            pl.BlockSpec((tb, DN_WIDTH), lambda b, i: (fwd(b, i), 0)),
            pl.BlockSpec((tb, DN_WIDTH), lambda b, i: (bwd(b, i), 0)),
            sspec,
        ],
        out_shape=[
            jax.ShapeDtypeStruct((rows, DN_WIDTH), BF16),
            jax.ShapeDtypeStruct((rows, DN_WIDTH), BF16),
            jax.ShapeDtypeStruct((batch, NPAIR, HEAD_DIM, PAIR), F32),
        ],
        scratch_shapes=[
            pltpu.VMEM((2, NPAIR, HEAD_DIM, PAIR), F32),
            pltpu.VMEM((nunit, CHUNK + HEAD_DIM, PAIR), BF16),
            pltpu.VMEM((nunit, 2 * CHUNK, PAIR), BF16),
            pltpu.VMEM((nunit, CHUNK, PAIR), F32),
            pltpu.VMEM((nunit, 8, PAIR), F32),
        ],
        compiler_params=pltpu.CompilerParams(
            dimension_semantics=("parallel", "arbitrary"), vmem_limit_bytes=VMEM_LIMIT),
        name="delta",
    )(qkv, qkv, qkv, kt, bgc, gt, qkv, qkv, qkv, kt, bgc, gt, s0)


def _phase_tables(num, den):
    ang = 2.0 * np.pi * (num % den).astype(np.float64) / den
    return np.cos(ang), np.sin(ang)


def _fft_factor(t):
    if t <= 4096:
        return t, 1
    best = None
    for n1 in range(16, t, 16):
        if t % n1:
            continue
        n2 = t // n1
        n2p = -(-n2 // T2_GROUP) * T2_GROUP
        cost = 4 * n1 + 2 * n2p
        if best is None or cost < best[0]:
            best = (cost, n1, n2)
    assert best is not None, t
    return best[1], best[2]


def _caxis_kernel(f_ref, cs_ref, a_ref, b_ref):
    cs = cs_ref[...]
    for g in range(FN_GROUPS):
        sl = slice(g * FN_GROUP_DIM, (g + 1) * FN_GROUP_DIM)
        ab = jnp.dot(f_ref[:, sl], cs, preferred_element_type=F32)
        a_ref[:, sl] = ab[:, :FN_GROUP_DIM].astype(a_ref.dtype)
        b_ref[:, sl] = ab[:, FN_GROUP_DIM:].astype(b_ref.dtype)


def _caxis(p, cs):
    rows = p.shape[0]
    tr = _row_tile(rows, 1024)
    spec = pl.BlockSpec((tr, FN_WIDTH), lambda i: (i, 0))
    return pl.pallas_call(
        _caxis_kernel,
        grid=(rows // tr,),
        in_specs=[pl.BlockSpec((tr, FN_WIDTH), lambda i: (i, PCOL_F)),
                  pl.BlockSpec((FN_GROUP_DIM, 2 * FN_GROUP_DIM), lambda i: (0, 0))],
        out_specs=[spec, spec],
        out_shape=[jax.ShapeDtypeStruct((rows, FN_WIDTH), BF16)] * 2,
        compiler_params=pltpu.CompilerParams(
            dimension_semantics=("parallel",), vmem_limit_bytes=VMEM_LIMIT),
        name="dft_channel",
    )(p, cs)


def _dft_dense_kernel(ax_ref, bx_ref, am_ref, bm_ref, cx_ref, sx_ref, cm_ref, sm_ref, o_ref, *, scale):
    dot = functools.partial(jnp.dot, preferred_element_type=F32)
    y = (dot(cx_ref[...], ax_ref[...]) - dot(sx_ref[...], bx_ref[...])
         + dot(cm_ref[...], am_ref[...]) - dot(sm_ref[...], bm_ref[...]))
    o_ref[...] = (y * scale).astype(o_ref.dtype)


def _dft_dense(a_x, b_x, a_m, b_m, *, batch, seq):
    t = N_META + seq
    scale = 1.0 / math.sqrt(t * FN_GROUP_DIM)
    k = np.arange(seq, dtype=np.int64)[:, None] + N_META
    cx, sx = _phase_tables(k * (np.arange(seq, dtype=np.int64)[None, :] + N_META), t)
    cm, sm = _phase_tables(k * np.arange(N_META, dtype=np.int64)[None, :], t)
    tmi = _row_tile(seq, 512)
    nm = seq // tmi
    xspec = pl.BlockSpec((seq, FN_WIDTH), lambda bb, m: (bb, 0))
    mspec = pl.BlockSpec((N_META, FN_WIDTH), lambda bb, m: (0, 0))
    return pl.pallas_call(
        functools.partial(_dft_dense_kernel, scale=scale),
        grid=(batch, nm),
        in_specs=[xspec, xspec, mspec, mspec,
                  pl.BlockSpec((tmi, seq), lambda bb, m: (m, 0)),
                  pl.BlockSpec((tmi, seq), lambda bb, m: (m, 0)),
                  pl.BlockSpec((tmi, N_META), lambda bb, m: (m, 0)),
                  pl.BlockSpec((tmi, N_META), lambda bb, m: (m, 0))],
        out_specs=pl.BlockSpec((tmi, FN_WIDTH), lambda bb, m: (bb * nm + m, 0)),
        out_shape=jax.ShapeDtypeStruct((batch * seq, FN_WIDTH), BF16),
        compiler_params=pltpu.CompilerParams(
            dimension_semantics=("parallel", "parallel"), vmem_limit_bytes=VMEM_LIMIT),
        name="dft_time_dense",
    )(a_x, b_x, a_m, b_m, jnp.asarray(cx, BF16), jnp.asarray(sx, BF16), jnp.asarray(cm, BF16), jnp.asarray(sm, BF16))


def _dft1_kernel(px_ref, pm_ref, cs_ref, c_ref, s_ref, twr_ref, twi_ref, yre_ref, yim_ref, scr, *, t, n1, n2):
    j = pl.program_id(1)
    halves = FN_GROUP_DIM // LANE

    @pl.when(j == 0)
    def _():
        for c in range(halves):
            ls = slice(c * LANE, (c + 1) * LANE)
            scr[c, 0:N_META, :] = pm_ref[:, ls].astype(F32)
            scr[c, N_META:t, :] = px_ref[:, ls].astype(F32)
            scr[c, t:t + T2_GROUP, :] = jnp.zeros((T2_GROUP, LANE), F32)

    dot = functools.partial(jnp.dot, preferred_element_type=F32)
    cs = cs_ref[...]
    c1 = c_ref[...]
    s1 = s_ref[...]
    xs = [jnp.concatenate([scr[c, pl.ds(j * T2_GROUP + tt, n1, stride=n2), :] for c in range(halves)],
                          axis=1).astype(BF16) for tt in range(T2_GROUP)]
    abs_ = [dot(x, cs) for x in xs]
    aa = [ab[:, :FN_GROUP_DIM].astype(BF16) for ab in abs_]
    bb = [ab[:, FN_GROUP_DIM:].astype(BF16) for ab in abs_]
    yre = [dot(c1, a) - dot(s1, b) for a, b in zip(aa, bb)]
    yim = [-(dot(c1, b) + dot(s1, a)) for a, b in zip(aa, bb)]
    for tt in range(T2_GROUP):
        live = j * T2_GROUP + tt < n2
        wr = twr_ref[tt]
        wi = twi_ref[tt]
        for c in range(halves):
            ls = slice(c * LANE, (c + 1) * LANE)
            re = yre[tt][:, ls]
            im = yim[tt][:, ls]
            yre_ref[tt, :, ls] = jnp.where(live, re * wr - im * wi, 0.0).astype(yre_ref.dtype)
            yim_ref[tt, :, ls] = jnp.where(live, re * wi + im * wr, 0.0).astype(yim_ref.dtype)


def _dft2_kernel(yre_ref, yim_ref, c_ref, s_ref, o_ref, re_scr, im_scr, o_scr, *, n2, n2p, scale):
    ncol = o_ref.shape[-1] // LANE

    def fill(i2, carry):
        rows = pl.ds(pl.multiple_of(i2 * 16, 16), 16)
        for c in range(ncol):
            ls = slice(c * LANE, (c + 1) * LANE)
            re_scr[c, rows, :] = yre_ref[i2, :, ls].astype(F32)
            im_scr[c, rows, :] = yim_ref[i2, :, ls].astype(F32)
        return carry

    lax.fori_loop(0, n2p, fill, 0)
    dot = functools.partial(jnp.dot, preferred_element_type=F32)
    c2 = c_ref[...]
    s2 = s_ref[...]
    for k in range(16):
        gather = lambda scr: jnp.concatenate(
            [scr[c, pl.ds(k, n2p, stride=16), :] for c in range(ncol)], axis=1).astype(BF16)
        y = (dot(c2, gather(re_scr)) + dot(s2, gather(im_scr))) * scale
        for c in range(ncol):
            o_scr[c, pl.ds(k, n2p, stride=16), :] = y[:, c * LANE:(c + 1) * LANE]

    def drain(i2, carry):
        rows = pl.ds(pl.multiple_of(i2 * 16, 16), 16)
        for c in range(ncol):
            o_ref[i2, :, c * LANE:(c + 1) * LANE] = o_scr[c, rows, :].astype(o_ref.dtype)
        return carry

    lax.fori_loop(0, n2, drain, 0)


def _dft_four_step(p, p_meta, cs, *, seq, factor=None):
    t = N_META + seq
    n1, n2 = factor if factor is not None else _fft_factor(t)
    assert n1 * n2 == t and n1 % 16 == 0 and n2 > 1
    n2p = -(-n2 // T2_GROUP) * T2_GROUP
    scale = 1.0 / math.sqrt(t * FN_GROUP_DIM)
    k1 = np.arange(n1, dtype=np.int64) + N_META
    c1, s1 = _phase_tables(k1[:, None] * np.arange(n1, dtype=np.int64)[None, :], n1)
    twc, tws = _phase_tables(np.arange(n2p, dtype=np.int64)[:, None] * k1[None, :], t)
    twr = jnp.asarray(np.broadcast_to(twc[:, :, None], (n2p, n1, LANE)), F32)
    twi = jnp.asarray(np.broadcast_to(-tws[:, :, None], (n2p, n1, LANE)), F32)
    fcol = lambda g, j: (0, PCOL_F * FN_GROUPS + g)
    const = lambda g, j: (0, 0)
    yre, yim = pl.pallas_call(
        functools.partial(_dft1_kernel, t=t, n1=n1, n2=n2),
        grid=(FN_GROUPS, n2p // T2_GROUP),
        in_specs=[
            pl.BlockSpec((seq, FN_GROUP_DIM), fcol),
            pl.BlockSpec((N_META, FN_GROUP_DIM), fcol),
            pl.BlockSpec((FN_GROUP_DIM, 2 * FN_GROUP_DIM), const),
            pl.BlockSpec((n1, n1), const),
            pl.BlockSpec((n1, n1), const),
            pl.BlockSpec((T2_GROUP, n1, LANE), lambda g, j: (j, 0, 0)),
            pl.BlockSpec((T2_GROUP, n1, LANE), lambda g, j: (j, 0, 0)),
        ],
        out_specs=[pl.BlockSpec((T2_GROUP, n1, FN_GROUP_DIM), lambda g, j: (j, 0, g))] * 2,
        out_shape=[jax.ShapeDtypeStruct((n2p, n1, FN_WIDTH), BF16)] * 2,
        scratch_shapes=[pltpu.VMEM((FN_GROUP_DIM // LANE, t + T2_GROUP, LANE), F32)],
        compiler_params=pltpu.CompilerParams(
            dimension_semantics=("parallel", "arbitrary"), vmem_limit_bytes=VMEM_LIMIT),
        name="dft_time_1",
    )(p, p_meta, cs, jnp.asarray(c1, BF16), jnp.asarray(s1, BF16), twr, twi)

    c2 = np.zeros((n2p, n2p))
    s2 = np.zeros((n2p, n2p))
    k2 = np.arange(n2, dtype=np.int64)
    c2[:n2, :n2], s2[:n2, :n2] = _phase_tables(k2[:, None] * k2[None, :], n2)
    tn = 512
    out = pl.pallas_call(
        functools.partial(_dft2_kernel, n2=n2, n2p=n2p, scale=scale),
        grid=(n1 // 16, FN_WIDTH // tn),
        in_specs=[
            pl.BlockSpec((n2p, 16, tn), lambda i, h: (0, i, h)),
            pl.BlockSpec((n2p, 16, tn), lambda i, h: (0, i, h)),
            pl.BlockSpec((n2p, n2p), lambda i, h: (0, 0)),
            pl.BlockSpec((n2p, n2p), lambda i, h: (0, 0)),
        ],
        out_specs=pl.BlockSpec((n2, 16, tn), lambda i, h: (0, i, h)),
        out_shape=jax.ShapeDtypeStruct((n2, n1, FN_WIDTH), BF16),
        scratch_shapes=[pltpu.VMEM((tn // LANE, n2p * 16, LANE), F32)] * 3,
        compiler_params=pltpu.CompilerParams(
            dimension_semantics=("parallel", "parallel"), vmem_limit_bytes=VMEM_LIMIT),
        name="dft_time_2",
    )(yre, yim, jnp.asarray(c2, BF16), jnp.asarray(s2, BF16))
    return out.reshape(t, FN_WIDTH)


def _merge_kernel(x_ref, hb_ref, st_ref, of_ref, ob_ref, za_ref, zf_ref, fr_ref, lig_ref, lib_ref, nw_ref,
                  wg_ref, bg_ref, wa_ref, wf_ref, wo_ref, lg_ref, lb_ref,
                  y_ref, ya_ref, yf_ref, m_ref, *, tn):
    dot = functools.partial(jnp.dot, preferred_element_type=F32)
    nw = nw_ref[...]
    for h in range(HEADS):
        hs = slice(h * HEAD_DIM, (h + 1) * HEAD_DIM)
        o = of_ref[:, hs].astype(F32) + ob_ref[:, hs].astype(F32)
        o = o * lax.rsqrt(jnp.mean(o * o, axis=-1, keepdims=True) + RMS_EPS) * nw
        ya_ref[:, hs] = (o * _silu(za_ref[:, hs].astype(F32))).astype(BF16)
    yf_ref[...] = (fr_ref[...].astype(F32) * _silu(zf_ref[...].astype(F32))).astype(BF16)

    hb = hb_ref[...]
    for j in range(D_MODEL // tn):
        ca = slice(j * tn, (j + 1) * tn)
        cf = slice(D_MODEL + j * tn, D_MODEL + (j + 1) * tn)
        ga = jax.nn.sigmoid(_dot_t(hb, wg_ref[ca, :]) + bg_ref[:, ca])
        gf = jax.nn.sigmoid(_dot_t(hb, wg_ref[cf, :]) + bg_ref[:, cf])
        m_ref[:, ca] = (ga * dot(ya_ref[...], wa_ref[:, ca]) + gf * dot(yf_ref[...], wf_ref[:, ca])).astype(BF16)
    out = dot(m_ref[...], wo_ref[...])

    st = st_ref[...]
    h = (x_ref[...] - st[:, BA_MU:BA_MU + 1]) * st[:, BA_RSTD:BA_RSTD + 1] * lig_ref[...] + lib_ref[...]
    y_ref[...] = _layer_norm(DEEPNORM_ALPHA * h + out, lg_ref[...], lb_ref[...])


def _merge(x, hb, ba, o_f, o_b, p, fr, ln_in_g, ln_in_b, nw, w_g, b_g, w_a, w_f, w_o, ln_g, ln_b, *,
           tm=256, tn=512):
    rows = x.shape[0]
    assert rows % tm == 0 and D_MODEL % tn == 0
    row_spec = lambda w, col=0: pl.BlockSpec((tm, w), lambda i: (i, col))
    return pl.pallas_call(
        functools.partial(_merge_kernel, tn=tn),
        grid=(rows // tm,),
        in_specs=[
            row_spec(D_MODEL), row_spec(D_MODEL), row_spec(LANE),
            row_spec(DN_WIDTH), row_spec(DN_WIDTH),
            row_spec(DN_WIDTH, PCOL_ZA), row_spec(FN_WIDTH, PCOL_ZF),
            row_spec(FN_WIDTH),
            _resident((1, D_MODEL)), _resident((1, D_MODEL)), _resident((1, HEAD_DIM)),
            _resident_rows(WROW_G, 2 * D_MODEL), _resident((1, 2 * D_MODEL)),
            _resident((DN_WIDTH, D_MODEL)), _resident((FN_WIDTH, D_MODEL)), _resident((D_MODEL, D_MODEL)),
            _resident((1, D_MODEL)), _resident((1, D_MODEL)),
        ],
        out_specs=pl.BlockSpec((tm, D_MODEL), lambda i: (i, 0)),
        out_shape=jax.ShapeDtypeStruct((rows, D_MODEL), F32),
        scratch_shapes=[
            pltpu.VMEM((tm, DN_WIDTH), BF16),
            pltpu.VMEM((tm, FN_WIDTH), BF16),
            pltpu.VMEM((tm, D_MODEL), BF16),
        ],
        compiler_params=pltpu.CompilerParams(
            dimension_semantics=("parallel",), vmem_limit_bytes=VMEM_LIMIT_RESIDENT),
        name="merge",
    )(x, hb, ba, o_f, o_b, p, p, fr, ln_in_g, ln_in_b, nw, w_g, b_g, w_a, w_f, w_o, ln_g, ln_b)


def _encode(x, p_meta, ba_meta, ab_meta, wts, *, fft_factor=None):
    batch, seq, _ = x.shape
    rows = batch * seq
    t_full = N_META + seq
    x2 = x.reshape(rows, D_MODEL)

    p, ba, hb = _ln_inproj(x2, wts["ln_in_g"], wts["ln_in_b"], wts["wt"],
                           tm=_row_tile(rows, INPROJ_ROWS))

    tb = DELTA_BLOCK
    assert seq % tb == 0 and seq >= META_BLOCK
    nt = seq // tb
    p4 = p.reshape(batch, nt, tb, P_WIDTH)
    tail = p4[:, :, tb - 2:, :3 * DN_WIDTH]
    head = p4[:, :, :2, :3 * DN_WIDTH]
    meta_tail = jnp.broadcast_to(p_meta[None, None, N_META - 2:, :3 * DN_WIDTH], (batch, 1, 2, 3 * DN_WIDTH))
    prev = jnp.concatenate([meta_tail, tail[:, :-1]], axis=1)
    nxt = jnp.concatenate([head[:, 1:], jnp.zeros((batch, 1, 2, 3 * DN_WIDTH), BF16)], axis=1)
    pad6 = jnp.zeros((batch, nt, 6, 3 * DN_WIDTH), BF16)
    halo = jnp.concatenate([pad6, prev, nxt, pad6], axis=2)
    qkv, kt, bgc, gt = _prep(p, halo, ba, wts["conv_w8"], wts["gpar"], batch=batch, nt=nt, tm=tb)

    mb = META_BLOCK
    p_m = jnp.concatenate([
        jnp.broadcast_to(p_meta[None, :, :3 * DN_WIDTH], (batch, N_META, 3 * DN_WIDTH)),
        p.reshape(batch, seq, P_WIDTH)[:, :mb - N_META, :3 * DN_WIDTH]], axis=1).reshape(batch * mb, 3 * DN_WIDTH)
    ba_m = jnp.concatenate([
        jnp.broadcast_to(ba_meta[None], (batch, N_META, LANE)),
        ba.reshape(batch, seq, LANE)[:, :mb - N_META]], axis=1).reshape(batch * mb, LANE)
    halo_m = jnp.zeros((batch, 1, 16, 3 * DN_WIDTH), BF16)
    qkv_m, kt_m, bgc_m, gt_m = _prep(p_m, halo_m, ba_m, wts["conv_w8"], wts["gpar"],
                                     batch=batch, nt=1, tm=mb, valid_rows=N_META)
    s_zero = jnp.zeros((batch, NPAIR, HEAD_DIM, PAIR), F32)
    _, _, s_meta = _delta(qkv_m, kt_m, bgc_m, gt_m, s_zero, batch=batch, nblk=1, tb=mb, ndir=1)

    o_f, o_b, _ = _delta(qkv, kt, bgc, gt, s_meta, batch=batch, nblk=nt, tb=tb)

    n1, n2 = fft_factor if fft_factor is not None else _fft_factor(t_full)
    if n2 == 1:
        a_x, b_x = _caxis(p, wts["cs"])
        fr = _dft_dense(a_x, b_x, ab_meta[0], ab_meta[1], batch=batch, seq=seq)
    else:
        assert batch == 1
        fr = _dft_four_step(p, p_meta, wts["cs"], seq=seq, factor=(n1, n2))

    y = _merge(x2, hb, ba, o_f, o_b, p, fr, wts["ln_in_g"], wts["ln_in_b"], wts["nw"], wts["wt"], wts["b_g"],
               wts["w_a"], wts["w_f"], wts["w_o"], wts["ln_g"], wts["ln_b"], tm=_row_tile(rows, MERGE_ROWS))
    return y.reshape(batch, seq, D_MODEL)


def _prepare_weights(ln_in_g, ln_in_b, w_in, b_gate, conv_w, a_log, dt_bias, dn_norm_w,
                     w_proj_a, w_proj_f, w_out, ln_g, ln_b):
    gpar = jnp.zeros((8, LANE), F32)
    gpar = gpar.at[0, BG_G:BG_GC].set(a_log[0].reshape(-1).astype(F32))
    gpar = gpar.at[1, BG_G:BG_GC].set(dt_bias[0].reshape(-1).astype(F32))
    k = np.arange(FN_GROUP_DIM, dtype=np.int64)
    ang = 2.0 * np.pi * ((k[:, None] * k[None, :]) % FN_GROUP_DIM) / FN_GROUP_DIM
    cs = jnp.asarray(np.concatenate([np.cos(ang), np.sin(ang)], axis=1), BF16)
    return {
        "ln_in_g": ln_in_g.reshape(1, D_MODEL).astype(F32),
        "ln_in_b": ln_in_b.reshape(1, D_MODEL).astype(F32),
        "wt": jnp.swapaxes(w_in[0], 0, 1).astype(BF16),
        "conv_w8": jnp.pad(conv_w[0].astype(F32), ((0, 8 - CONV_K), (0, 0))),
        "gpar": gpar,
        "cs": cs,
        "nw": dn_norm_w[0].reshape(1, HEAD_DIM).astype(F32),
        "b_g": b_gate[0].reshape(1, 2 * D_MODEL).astype(F32),
        "w_a": w_proj_a[0].astype(BF16),
        "w_f": w_proj_f[0].astype(BF16),
        "w_o": w_out[0].astype(BF16),
        "ln_g": ln_g[0].reshape(1, D_MODEL).astype(F32),
        "ln_b": ln_b[0].reshape(1, D_MODEL).astype(F32),
    }


def kernel(x_prompt, x_sample, meta_tokens, ln_in_g, ln_in_b, w_in, b_gate, conv_w, a_log, dt_bias, dn_norm_w,
           w_proj_a, w_proj_f, w_out, ln_g, ln_b):
    wts = _prepare_weights(ln_in_g, ln_in_b, w_in, b_gate, conv_w, a_log, dt_bias, dn_norm_w,
                           w_proj_a, w_proj_f, w_out, ln_g, ln_b)
    p_meta, ba_meta, _ = _ln_inproj(meta_tokens.astype(F32), wts["ln_in_g"], wts["ln_in_b"],
                                    wts["wt"], tm=N_META)
    ab_meta = _caxis(p_meta, wts["cs"])
    y_prompt = _encode(x_prompt, p_meta, ba_meta, ab_meta, wts)
    y_sample = _encode(x_sample, p_meta, ba_meta, ab_meta, wts)
    return (y_prompt, y_sample)
```
